```python
import math
import jax, jax.numpy as jnp
from jax import lax
import numpy as np

D_MODEL = 1024
BATCH = 4
SEQ = 4096
DEPTH = 2
DEC_BATCH = 128
DEC_SEQ = 4
PAST_LEN = 2048
PAGE_SIZE = 128

HEAD_DIM = 64
N_HEADS = D_MODEL // HEAD_DIM
N_MIXERS = 2
N_LAYERS_A = (DEPTH + 1) // 2
N_LAYERS_B = DEPTH // 2
DECAY_LORA = 64
AAA_LORA = 64
GATE_LORA = 128
GN_EPS = 64e-5
WINDOWS = (128, 512, 2048)
DILATIONS = (1, 4, 16)
N_GROUPS = 3
ROPE_THETA = 10000.0
D_FF = 2816
CONV_W = 3
NORM_EPS = 1e-6

kernel_name = "rwkv7_dilated_swa_convglu_step"

F32 = jnp.float32


def rmsnorm(x, g):
    x32 = x.astype(F32)
    y = x32 * lax.rsqrt(jnp.mean(x32 * x32, axis=-1, keepdims=True) + NORM_EPS)
    return (y * g).astype(x.dtype)


def rope(x, pos):
    half = HEAD_DIM // 2
    inv = jnp.exp(-math.log(ROPE_THETA) * jnp.arange(half, dtype=F32) * 2.0 / HEAD_DIM)
    ang = pos.astype(F32)[:, None] * inv[None, :]
    cos = jnp.cos(ang)[None, :, None, :]
    sin = jnp.sin(ang)[None, :, None, :]
    x1 = x[..., :half].astype(F32)
    x2 = x[..., half:].astype(F32)
    return jnp.concatenate([x1 * cos - x2 * sin, x2 * cos + x1 * sin], axis=-1)


def wkv7_scan(r, w, k, v, a, b, S0):
    def step(S, inp):
        r_t, w_t, k_t, v_t, a_t, b_t = inp
        sa = jnp.einsum('bhvk,bhk->bhv', S, a_t)
        S = S * w_t[:, :, None, :] + sa[..., None] * b_t[:, :, None, :] + v_t[..., None] * k_t[:, :, None, :]
        y = jnp.einsum('bhvk,bhk->bhv', S, r_t)
        return S, y
    seq = tuple(jnp.moveaxis(t, 1, 0) for t in (r, w, k, v, a, b))
    S_T, ys = lax.scan(step, S0, seq)
    return jnp.moveaxis(ys, 0, 1), S_T


def rwkv7_time_mix(xn, x_last, S0, mu, w_rkv, w0, w1, w2, a0, a1, a2, g1, g2, k_k, k_a, r_k, ln_w, ln_b, w_o):
    B, T, D = xn.shape
    x_prev = jnp.concatenate([x_last[:, None, :].astype(xn.dtype), xn[:, :-1]], axis=1)
    xx = x_prev - xn
    xs = xn[:, :, None, :] + xx[:, :, None, :] * mu
    rkv = jnp.einsum('btnc,ncd->btnd', xs[:, :, :3], w_rkv)
    r, k, v = rkv[:, :, 0], rkv[:, :, 1], rkv[:, :, 2]
    xw, xa, xg = xs[:, :, 3], xs[:, :, 4], xs[:, :, 5]
    w_log = -jax.nn.softplus(-(w0 + jnp.tanh(xw @ w1) @ w2)) - 0.5
    decay = jnp.exp(-jnp.exp(w_log.astype(F32)))
    a = jax.nn.sigmoid(a0 + (xa @ a1) @ a2)
    g = jax.nn.sigmoid(xg @ g1) @ g2

    def heads(t):
        return t.reshape(B, T, N_HEADS, HEAD_DIM).astype(F32)

    kk = heads(k * k_k)
    kk = kk / jnp.maximum(jnp.sqrt(jnp.sum(kk * kk, axis=-1, keepdims=True)), 1e-12)
    k = k * (1.0 + (a - 1.0) * k_a)
    rh, kh, vh, ah = heads(r), heads(k), heads(v), heads(a)
    y, S_T = wkv7_scan(rh, heads(decay), kh, vh, -kk, kk * ah, S0.astype(F32))
    mean = jnp.mean(y, axis=-1, keepdims=True)
    var = jnp.mean(jnp.square(y - mean), axis=-1, keepdims=True)
    y = ((y - mean) * lax.rsqrt(var + GN_EPS)).reshape(B, T, D) * ln_w + ln_b
    bonus = jnp.sum(rh * kh * r_k, axis=-1, keepdims=True) * vh
    y = y + bonus.reshape(B, T, D)
    out = (y * g) @ w_o
    return out.astype(xn.dtype), xn[:, -1], S_T


def band_dilated_attention(q, k, v, dil, n_keys):
    B, T, H, Dh = q.shape
    L = T // dil
    blk = n_keys
    nb = -(-L // blk)
    Lp = nb * blk

    def cls(t):
        t = t.reshape(B, L, dil, H, Dh).transpose(0, 2, 1, 3, 4)
        t = jnp.pad(t, ((0, 0), (0, 0), (0, Lp - L), (0, 0), (0, 0)))
        return t.reshape(B, dil, nb, blk, H, Dh)

    qb, kb, vb = cls(q), cls(k), cls(v.astype(F32))

    def prev_block(t):
        return jnp.concatenate([jnp.zeros_like(t[:, :, :1]), t[:, :, :-1]], axis=2)

    kc = jnp.concatenate([prev_block(kb), kb], axis=3)
    vc = jnp.concatenate([prev_block(vb), vb], axis=3)
    s = jnp.einsum('bcnqhd,bcnkhd->bcnhqk', qb, kc).astype(F32) * (Dh ** -0.5)
    i = jnp.arange(blk)[:, None]
    j = jnp.arange(2 * blk)[None, :]
    dist = blk + i - j
    bidx = jnp.arange(nb)[:, None, None]
    valid = (dist >= 0) & (dist <= n_keys) & ((bidx > 0) | (j >= blk))
    s = jnp.where(valid[None, None, :, None, :, :], s, -jnp.inf)
    m = jnp.max(s, axis=-1, keepdims=True)
    p = jnp.exp(s - m)
    den = jnp.sum(p, axis=-1)
    o = jnp.einsum('bcnhqk,bcnkhd->bcnqhd', p, vc) / jnp.swapaxes(den, -1, -2)[..., None]
    lse = m[..., 0] + jnp.log(den)
    o = o.reshape(B, dil, Lp, H, Dh)[:, :, :L].transpose(0, 2, 1, 3, 4).reshape(B, T, H, Dh)
    lse = jnp.swapaxes(lse, -1, -2).reshape(B, dil, Lp, H)[:, :, :L].transpose(0, 2, 1, 3).reshape(B, T, H)
    return o, lse


def gathered_dilated_attention(q, k_all, v_all, dil, n_keys, n_past):
    S = q.shape[1]
    idx = n_past + jnp.arange(S)[:, None] - dil * jnp.arange(n_keys + 1)[None, :]
    valid = idx >= 0
    idx_c = jnp.maximum(idx, 0)
    kg = k_all[:, idx_c]
    vg = v_all[:, idx_c].astype(F32)
    s = jnp.einsum('bshd,bsmhd->bshm', q, kg).astype(F32) * (HEAD_DIM ** -0.5)
    s = jnp.where(valid[None, :, None, :], s, -jnp.inf)
    m = jnp.max(s, axis=-1, keepdims=True)
    p = jnp.exp(s - m)
    den = jnp.sum(p, axis=-1)
    o = jnp.einsum('bshm,bsmhd->bshd', p, vg) / den[..., None]
    return o, m[..., 0] + jnp.log(den)


def combine_by_denominators(outs, lses):
    wts = jax.nn.softmax(jnp.stack(lses, axis=0), axis=0)
    return jnp.sum(wts[..., None] * jnp.stack(outs, axis=0), axis=0)


def split_qkv(xn, w_in):
    B, T, _ = xn.shape
    return (xn @ w_in).reshape(B, T, N_GROUPS, 3, N_HEADS, HEAD_DIM)


def attn_prompt(xn, w_in, w_o):
    B, T, _ = xn.shape
    qkv = split_qkv(xn, w_in)
    pos = jnp.arange(T)
    outs, lses, ks, vs = [], [], [], []
    for g in range(N_GROUPS):
        q = rope(qkv[:, :, g, 0], pos)
        k = rope(qkv[:, :, g, 1], pos)
        v = qkv[:, :, g, 2]
        o, l = band_dilated_attention(q, k, v, DILATIONS[g], WINDOWS[g] // DILATIONS[g])
        outs.append(o)
        lses.append(l)
        keep = min(WINDOWS[g], T)
        ks.append(k[:, T - keep:])
        vs.append(v[:, T - keep:])
    o = combine_by_denominators(outs, lses)
    return (o.reshape(B, T, -1) @ w_o).astype(xn.dtype), ks, vs


def attn_sample(xn, k_caches, v_caches, w_in, w_o):
    B, S, _ = xn.shape
    qkv = split_qkv(xn, w_in)
    pos = PAST_LEN + jnp.arange(S)
    outs, lses, ks, vs = [], [], [], []
    for g in range(N_GROUPS):
        q = rope(qkv[:, :, g, 0], pos)
        k = rope(qkv[:, :, g, 1], pos)
        v = qkv[:, :, g, 2]
        n_past = k_caches[g].shape[1]
        k_all = jnp.concatenate([k_caches[g], k], axis=1)
        v_all = jnp.concatenate([v_caches[g], v], axis=1)
        o, l = gathered_dilated_attention(q, k_all, v_all, DILATIONS[g], WINDOWS[g] // DILATIONS[g], n_past)
        outs.append(o)
        lses.append(l)
        ks.append(k)
        vs.append(v)
    o = combine_by_denominators(outs, lses)
    return (o.reshape(B, S, -1) @ w_o).astype(xn.dtype), ks, vs


def conv_glu_ffn(xn, buf, w_up, conv_w, conv_b, w_down):
    T = xn.shape[1]
    u = xn @ w_up
    gate, val = u[..., :D_FF], u[..., D_FF:]
    ext = jnp.concatenate([buf.astype(gate.dtype), gate], axis=1)
    conv = conv_b + sum(ext[:, i:i + T] * conv_w[i] for i in range(CONV_W))
    out = (jax.nn.silu(conv) * val) @ w_down
    return out.astype(xn.dtype), ext[:, -(CONV_W - 1):]


def setup_inputs(seed: int = 0) -> dict:
    key = jax.random.key(seed)
    ks = iter(jax.random.split(key, 48))
    D, H, Dh = D_MODEL, N_HEADS, HEAD_DIM

    def nrm(shape, scale=1.0):
        return jax.random.normal(next(ks), shape, F32) * scale

    def gain(shape):
        return 1.0 + nrm(shape, 0.05)

    inp = {}
    inp["x_prompt"] = nrm((BATCH, SEQ, D))
    inp["x_sample"] = nrm((DEC_BATCH, DEC_SEQ, D))
    inp["state_rwkv_shift"] = nrm((N_LAYERS_A, DEC_BATCH, D))
    inp["state_rwkv_wkv"] = nrm((N_LAYERS_A, DEC_BATCH, H, Dh, Dh), 0.5)
    for w in WINDOWS:
        rows = min(w, PAST_LEN)
        inp["cache_k_w%d" % w] = nrm((N_LAYERS_B, DEC_BATCH, rows, H, Dh))
        inp["cache_v_w%d" % w] = nrm((N_LAYERS_B, DEC_BATCH, rows, H, Dh))
    inp["state_ffn_conv"] = nrm((DEPTH, DEC_BATCH, CONV_W - 1, D_FF))
    inp["norm_mix"] = gain((DEPTH, D))
    inp["norm_ffn"] = gain((DEPTH, D))
    inp["norm_final"] = gain((D,))
    inp["rwkv_mu"] = jax.random.uniform(next(ks), (N_LAYERS_A, 6, D), F32)
    inp["rwkv_w_rkv"] = nrm((N_LAYERS_A, 3, D, D), D ** -0.5)
    inp["rwkv_w0"] = jax.random.uniform(next(ks), (N_LAYERS_A, D), F32, -4.0, 0.0)
    inp["rwkv_w1"] = nrm((N_LAYERS_A, D, DECAY_LORA), D ** -0.5)
    inp["rwkv_w2"] = nrm((N_LAYERS_A, DECAY_LORA, D), 0.5 * DECAY_LORA ** -0.5)
    inp["rwkv_a0"] = nrm((N_LAYERS_A, D), 0.5)
    inp["rwkv_a1"] = nrm((N_LAYERS_A, D, AAA_LORA), D ** -0.5)
    inp["rwkv_a2"] = nrm((N_LAYERS_A, AAA_LORA, D), 0.5 * AAA_LORA ** -0.5)
    inp["rwkv_g1"] = nrm((N_LAYERS_A, D, GATE_LORA), D ** -0.5)
    inp["rwkv_g2"] = nrm((N_LAYERS_A, GATE_LORA, D), GATE_LORA ** -0.5)
    inp["rwkv_k_k"] = 0.85 + nrm((N_LAYERS_A, D), 0.05)
    inp["rwkv_k_a"] = 1.0 + nrm((N_LAYERS_A, D), 0.05)
    inp["rwkv_r_k"] = nrm((N_LAYERS_A, H, Dh), 0.5)
    inp["rwkv_ln_w"] = gain((N_LAYERS_A, D))
    inp["rwkv_ln_b"] = nrm((N_LAYERS_A, D), 0.02)
    inp["rwkv_w_o"] = nrm((N_LAYERS_A, D, D), D ** -0.5)
    inp["attn_w_in"] = nrm((N_LAYERS_B, D, N_GROUPS * 3 * H * Dh), D ** -0.5)
    inp["attn_w_o"] = nrm((N_LAYERS_B, H * Dh, D), (H * Dh) ** -0.5)
    inp["ffn_w_up"] = nrm((DEPTH, D, 2 * D_FF), D ** -0.5)
    inp["ffn_conv_w"] = nrm((DEPTH, CONV_W, D_FF), CONV_W ** -0.5)
    inp["ffn_conv_b"] = nrm((DEPTH, D_FF), 0.02)
    inp["ffn_w_down"] = nrm((DEPTH, D_FF, D), D_FF ** -0.5)
    return inp


def reference(x_prompt, x_sample, state_rwkv_shift, state_rwkv_wkv,
              cache_k_w128, cache_v_w128, cache_k_w512, cache_v_w512, cache_k_w2048, cache_v_w2048,
              state_ffn_conv, norm_mix, norm_ffn, norm_final,
              rwkv_mu, rwkv_w_rkv, rwkv_w0, rwkv_w1, rwkv_w2, rwkv_a0, rwkv_a1, rwkv_a2,
              rwkv_g1, rwkv_g2, rwkv_k_k, rwkv_k_a, rwkv_r_k, rwkv_ln_w, rwkv_ln_b, rwkv_w_o,
              attn_w_in, attn_w_o, ffn_w_up, ffn_conv_w, ffn_conv_b, ffn_w_down):
    Bp = x_prompt.shape[0]
    xp, xs = x_prompt, x_sample
    p_shift, p_wkv, s_shift, s_wkv = [], [], [], []
    p_k = [[] for _ in range(N_GROUPS)]
    p_v = [[] for _ in range(N_GROUPS)]
    s_k = [[] for _ in range(N_GROUPS)]
    s_v = [[] for _ in range(N_GROUPS)]
    p_conv, s_conv = [], []
    for i in range(DEPTH):
        hp = rmsnorm(xp, norm_mix[i])
        hs = rmsnorm(xs, norm_mix[i])
        li = i // N_MIXERS
        if i % N_MIXERS == 0:
            prm = (rwkv_mu[li], rwkv_w_rkv[li], rwkv_w0[li], rwkv_w1[li], rwkv_w2[li],
                   rwkv_a0[li], rwkv_a1[li], rwkv_a2[li], rwkv_g1[li], rwkv_g2[li],
                   rwkv_k_k[li], rwkv_k_a[li], rwkv_r_k[li], rwkv_ln_w[li], rwkv_ln_b[li], rwkv_w_o[li])
            zero_shift = jnp.zeros((Bp, D_MODEL), F32)
            zero_wkv = jnp.zeros((Bp, N_HEADS, HEAD_DIM, HEAD_DIM), F32)
            mp, shp, stp = rwkv7_time_mix(hp, zero_shift, zero_wkv, *prm)
            ms, shs, sts = rwkv7_time_mix(hs, state_rwkv_shift[li], state_rwkv_wkv[li], *prm)
            p_shift.append(shp)
            p_wkv.append(stp)
            s_shift.append(shs)
            s_wkv.append(sts)
        else:
            mp, kp, vp = attn_prompt(hp, attn_w_in[li], attn_w_o[li])
            ms, kn, vn = attn_sample(hs, (cache_k_w128[li], cache_k_w512[li], cache_k_w2048[li]),
                                     (cache_v_w128[li], cache_v_w512[li], cache_v_w2048[li]),
                                     attn_w_in[li], attn_w_o[li])
            for g in range(N_GROUPS):
                p_k[g].append(kp[g])
                p_v[g].append(vp[g])
                s_k[g].append(kn[g])
                s_v[g].append(vn[g])
        xp = xp + mp
        xs = xs + ms
        hp = rmsnorm(xp, norm_ffn[i])
        hs = rmsnorm(xs, norm_ffn[i])
        fp, bp = conv_glu_ffn(hp, jnp.zeros((Bp, CONV_W - 1, D_FF), F32),
                              ffn_w_up[i], ffn_conv_w[i], ffn_conv_b[i], ffn_w_down[i])
        fs, bs = conv_glu_ffn(hs, state_ffn_conv[i], ffn_w_up[i], ffn_conv_w[i], ffn_conv_b[i], ffn_w_down[i])
        p_conv.append(bp)
        s_conv.append(bs)
        xp = xp + fp
        xs = xs + fs
    y_prompt = rmsnorm(xp, norm_final)
    y_sample = rmsnorm(xs, norm_final)
    return (y_prompt, y_sample,
            jnp.stack(p_shift), jnp.stack(p_wkv),
            jnp.stack(p_k[0]), jnp.stack(p_v[0]), jnp.stack(p_k[1]), jnp.stack(p_v[1]),
            jnp.stack(p_k[2]), jnp.stack(p_v[2]), jnp.stack(p_conv),
            jnp.stack(s_shift), jnp.stack(s_wkv),
            jnp.stack(s_k[0]), jnp.stack(s_v[0]), jnp.stack(s_k[1]), jnp.stack(s_v[1]),
            jnp.stack(s_k[2]), jnp.stack(s_v[2]), jnp.stack(s_conv))
```

```python
import functools
import math

import jax
import jax.numpy as jnp
from jax import lax
from jax.experimental import pallas as pl
from jax.experimental.pallas import tpu as pltpu

F32 = jnp.float32
BF16 = jnp.bfloat16

D_MODEL = 1024
HEAD_DIM = 64
N_HEADS = 16
D_FF = 2816
CONV_W = 3
N_GROUPS = 3
WINDOWS = (128, 512, 2048)
DILATIONS = (1, 4, 16)
BAND = 128
GN_EPS = 64e-5
NORM_EPS = 1e-6
ROPE_THETA = 10000.0
PAST_LEN = 2048
LANES = 128
SUBLANES = 8
VMEM_LIMIT = 56 * 1024 * 1024


def _params(*sem):
    return pltpu.CompilerParams(dimension_semantics=sem, vmem_limit_bytes=VMEM_LIMIT)


def _mm_kernel(x_ref, w_ref, o_ref):
    o_ref[...] = jnp.dot(x_ref[...].astype(BF16), w_ref[...],
                         preferred_element_type=F32).astype(o_ref.dtype)


def mm(x, w, tm=512, tn=512):
    M, K = x.shape
    N = w.shape[1]
    tm = min(tm, M)
    tn = min(tn, N)
    assert M % tm == 0 and N % tn == 0
    return pl.pallas_call(
        _mm_kernel,
        grid=(M // tm, N // tn),
        in_specs=[pl.BlockSpec((tm, K), lambda i, j: (i, 0)),
                  pl.BlockSpec((K, tn), lambda i, j: (0, j))],
        out_specs=pl.BlockSpec((tm, tn), lambda i, j: (i, j)),
        out_shape=jax.ShapeDtypeStruct((M, N), F32),
        compiler_params=_params("parallel", "parallel"),
        name="mm",
    )(x, w.astype(BF16))


def _wkv_kernel(a_ref, wr_ref, w_ref, b_ref, k_ref, v_ref, br_ref, kr_ref, s0_ref,
                y_ref, st_ref, state, *, steps):
    @pl.when(pl.program_id(1) == 0)
    def _():
        state[...] = s0_ref[...]

    def row(ref, t, k):
        return jnp.broadcast_to(ref[t, pl.ds(k, 1), :], (SUBLANES, LANES))[None]

    def step(t, carry):
        sa = jnp.zeros((4, SUBLANES, LANES), F32)
        yy = jnp.zeros((4, SUBLANES, LANES), F32)
        for k in range(HEAD_DIM):
            sk = state[k]
            sa = sa + sk * row(a_ref, t, k)
            yy = yy + sk * row(wr_ref, t, k)
        v = v_ref[t]
        y_ref[t] = yy + sa * row(br_ref, t, 0) + v * row(kr_ref, t, 0)
        for k in range(HEAD_DIM):
            state[k] = state[k] * row(w_ref, t, k) + sa * row(b_ref, t, k) + v * row(k_ref, t, k)
        return carry

    lax.fori_loop(0, steps, step, 0)

    @pl.when(pl.program_id(1) == pl.num_programs(1) - 1)
    def _():
        st_ref[...] = state[...]


def wkv_scan(rows, v, br, kr, s0, steps):
    T, _, L = rows[0].shape
    nl = L // LANES
    assert T % steps == 0
    row_spec = pl.BlockSpec((steps, HEAD_DIM, LANES), lambda l, t: (t, 0, l))
    col_spec = pl.BlockSpec((steps, 4, SUBLANES, LANES), lambda l, t: (t, 0, 0, l))
    sc_spec = pl.BlockSpec((steps, 1, LANES), lambda l, t: (t, 0, l))
    st_spec = pl.BlockSpec((HEAD_DIM, 4, SUBLANES, LANES), lambda l, t: (0, 0, 0, l))
    return pl.pallas_call(
        functools.partial(_wkv_kernel, steps=steps),
        grid=(nl, T // steps),
        in_specs=[row_spec] * 5 + [col_spec, sc_spec, sc_spec, st_spec],
        out_specs=[col_spec, st_spec],
        out_shape=[jax.ShapeDtypeStruct(v.shape, F32), jax.ShapeDtypeStruct(s0.shape, F32)],
        scratch_shapes=[pltpu.VMEM((HEAD_DIM, 4, SUBLANES, LANES), F32)],
        compiler_params=_params("parallel", "arbitrary"),
        name="wkv_scan",
    )(*rows, v, br, kr, s0)


def _to_rows(x, B, T):
    nl = B * N_HEADS // 64
    x = x.reshape(B, T, N_HEADS, HEAD_DIM).transpose(1, 3, 0, 2).reshape(T, HEAD_DIM, nl, 1, 64)
    return jnp.broadcast_to(x, (T, HEAD_DIM, nl, 2, 64)).reshape(T, HEAD_DIM, nl * LANES)


def _to_cols(x, B, T):
    nl = B * N_HEADS // 64
    x = x.reshape(B, T, N_HEADS, 2, 32).transpose(1, 4, 0, 2, 3).reshape(T, 32, nl, 64, 2)
    return x.transpose(0, 1, 2, 4, 3).reshape(T, 4, SUBLANES, nl * LANES)


def _from_cols(y, B, T):
    nl = B * N_HEADS // 64
    y = y.reshape(T, 32, nl, 2, 64).transpose(0, 2, 4, 3, 1)
    return y.reshape(T, B, N_HEADS, HEAD_DIM).transpose(1, 0, 2, 3).reshape(B * T, D_MODEL)


def _head_scalar(x, B, T):
    nl = B * N_HEADS // 64
    x = x.reshape(B, T, N_HEADS).transpose(1, 0, 2).reshape(T, nl, 1, 64)
    return jnp.broadcast_to(x, (T, nl, 2, 64)).reshape(T, 1, nl * LANES)


def _state_in(s, B):
    nl = B * N_HEADS // 64
    s = s.reshape(B, N_HEADS, 2, 32, HEAD_DIM).transpose(4, 3, 0, 1, 2).reshape(HEAD_DIM, 32, nl, 64, 2)
    return s.transpose(0, 1, 2, 4, 3).reshape(HEAD_DIM, 4, SUBLANES, nl * LANES)


def _state_out(s, B):
    nl = B * N_HEADS // 64
    s = s.reshape(HEAD_DIM, 32, nl, 2, 64).transpose(2, 4, 3, 1, 0)
    return s.reshape(B, N_HEADS, HEAD_DIM, HEAD_DIM)


def _wkv(r, decay, k, v, a, b, s0, B, T, steps):
    hsum = lambda x: x.reshape(B * T, N_HEADS, HEAD_DIM).sum(-1)
    rows = [_to_rows(x, B, T) for x in (a, decay * r, decay, b, k)]
    y, st = wkv_scan(rows, _to_cols(v, B, T), _head_scalar(hsum(b * r), B, T),
                     _head_scalar(hsum(k * r), B, T), _state_in(s0, B), steps)
    return _from_cols(y, B, T), _state_out(st, B)


def _band_attn_kernel(q_ref, kc_ref, kp_ref, vc_ref, vp_ref, o_ref, lse_ref):
    n = pl.program_id(2)
    i = lax.broadcasted_iota(jnp.int32, (BAND, 2 * BAND), 0)
    j = lax.broadcasted_iota(jnp.int32, (BAND, 2 * BAND), 1)
    valid = (j >= i) & (j <= i + BAND) & ((n > 0) | (j >= BAND))
    lses = []
    for h in range(N_HEADS):
        sl = slice(h * HEAD_DIM, (h + 1) * HEAD_DIM)
        q = q_ref[0, :, sl].astype(BF16)
        k = jnp.concatenate([kp_ref[0, :, sl], kc_ref[0, :, sl]], axis=0).astype(BF16)
        v = jnp.concatenate([vp_ref[0, :, sl], vc_ref[0, :, sl]], axis=0).astype(BF16)
        s = lax.dot_general(q, k, (((1,), (1,)), ((), ())), preferred_element_type=F32)
        s = jnp.where(valid, s * (HEAD_DIM ** -0.5), -jnp.inf)
        m = jnp.max(s, axis=-1, keepdims=True)
        p = jnp.exp(s - m)
        den = jnp.sum(p, axis=-1, keepdims=True)
        o = jnp.dot(p.astype(BF16), v, preferred_element_type=F32)
        o_ref[0, :, sl] = o / den
        lses.append(m + jnp.log(den))
    lse_ref[0, 0] = jnp.concatenate(lses, axis=1)


def band_attn(q, k, v, dil):
    B, T, D = q.shape
    L = T // dil
    nb = L // BAND
    view = lambda x: x.reshape(B, L, dil * D)
    cur = pl.BlockSpec((1, BAND, D), lambda b, c, n: (b, n, c))
    prev = pl.BlockSpec((1, BAND, D), lambda b, c, n: (b, jnp.maximum(n - 1, 0), c))
    o, lse = pl.pallas_call(
        _band_attn_kernel,
        grid=(B, dil, nb),
        in_specs=[cur, cur, prev, cur, prev],
        out_specs=[cur, pl.BlockSpec((1, 1, BAND, N_HEADS), lambda b, c, n: (b, c, n, 0))],
        out_shape=[jax.ShapeDtypeStruct((B, L, dil * D), F32),
                   jax.ShapeDtypeStruct((B, dil, L, N_HEADS), F32)],
        compiler_params=_params("parallel", "parallel", "arbitrary"),
        name="band_attn",
    )(view(q), view(k), view(k), view(v), view(v))
    return o.reshape(B, T, D), lse.transpose(0, 2, 1, 3).reshape(B, T, N_HEADS)


def _sample_attn_kernel(q_ref, kn_ref, vn_ref, ck0_ref, cv0_ref, ck1_ref, cv1_ref, ck2_ref, cv2_ref,
                        seg_ref, segt_ref, o_ref, *, n_new):
    seg = seg_ref[...]
    segt = segt_ref[...]
    scale = HEAD_DIM ** -0.5
    crow = lax.broadcasted_iota(jnp.int32, (BAND, N_HEADS), 0)
    nrow = lax.broadcasted_iota(jnp.int32, (SUBLANES, N_HEADS), 0)
    caches = ((ck0_ref, cv0_ref), (ck1_ref, cv1_ref), (ck2_ref, cv2_ref))

    def expand(x):
        hi = x.astype(BF16)
        lo = (x - hi.astype(F32)).astype(BF16)
        return (jnp.dot(hi, segt, preferred_element_type=F32)
                + jnp.dot(lo, segt, preferred_element_type=F32))

    for j in range(n_new):
        ms, dens, accs = [], [], []
        for g in range(N_GROUPS):
            gs = slice(g * D_MODEL, (g + 1) * D_MODEL)
            q = q_ref[0, j:j + 1, gs]
            ck_ref, cv_ref = caches[g]
            if g == 0:
                kc, vc = ck_ref[0], cv_ref[0]
                cmask = crow >= j
                nmask = nrow <= j
            else:
                cs = slice(j * D_MODEL, (j + 1) * D_MODEL)
                kc, vc = ck_ref[0, :, cs], cv_ref[0, :, cs]
                cmask = crow >= 0
                nmask = nrow == j
            kn, vn = kn_ref[0, :, gs], vn_ref[0, :, gs]
            sc = jnp.dot((kc * q).astype(BF16), seg, preferred_element_type=F32) * scale
            sn = jnp.dot((kn * q).astype(BF16), seg, preferred_element_type=F32) * scale
            sc = jnp.where(cmask, sc, -jnp.inf)
            sn = jnp.where(nmask, sn, -jnp.inf)
            m = jnp.maximum(jnp.max(sc, axis=0, keepdims=True), jnp.max(sn, axis=0, keepdims=True))
            pc = jnp.exp(sc - m)
            pn = jnp.exp(sn - m)
            den = jnp.sum(pc, axis=0, keepdims=True) + jnp.sum(pn, axis=0, keepdims=True)
            acc = (jnp.sum(expand(pc) * vc, axis=0, keepdims=True)
                   + jnp.sum(expand(pn) * vn, axis=0, keepdims=True))
            ms.append(m)
            dens.append(den)
            accs.append(acc)
        mx = jnp.maximum(jnp.maximum(ms[0], ms[1]), ms[2])
        fs = [jnp.exp(m - mx) for m in ms]
        tot = fs[0] * dens[0] + fs[1] * dens[1] + fs[2] * dens[2]
        out = jnp.zeros((1, D_MODEL), F32)
        for g in range(N_GROUPS):
            wgt = jnp.broadcast_to(fs[g] / tot, (SUBLANES, N_HEADS))
            out = out + expand(wgt)[0:1] * accs[g]
        o_ref[0, j:j + 1, :] = out


def sample_attn(q, kn, vn, caches_k, caches_v, n_new):
    B = q.shape[0]
    D = D_MODEL
    seg = (jnp.arange(D)[:, None] // HEAD_DIM == jnp.arange(N_HEADS)[None, :]).astype(BF16)
    new_spec = pl.BlockSpec((1, SUBLANES, N_GROUPS * D), lambda b: (b, 0, 0))
    specs = [new_spec] * 3
    args = [q, kn, vn]
    for g in range(N_GROUPS):
        dil = DILATIONS[g]
        rows = caches_k[g].shape[1]
        assert rows == BAND * dil and (g == 0 or n_new <= dil)
        width = D if g == 0 else n_new * D
        spec = pl.BlockSpec((1, BAND, width), lambda b: (b, 0, 0))
        specs += [spec, spec]
        args += [caches_k[g].reshape(B, BAND, dil * D), caches_v[g].reshape(B, BAND, dil * D)]
    specs += [pl.BlockSpec((D, N_HEADS), lambda b: (0, 0)), pl.BlockSpec((N_HEADS, D), lambda b: (0, 0))]
    args += [seg, seg.T]
    return pl.pallas_call(
        functools.partial(_sample_attn_kernel, n_new=n_new),
        grid=(B,),
        in_specs=specs,
        out_specs=pl.BlockSpec((1, n_new, D), lambda b: (b, 0, 0)),
        out_shape=jax.ShapeDtypeStruct((B, n_new, D), F32),
        compiler_params=_params("parallel"),
        name="sample_attn",
    )(*args)


def _rmsnorm(x, g):
    return x * lax.rsqrt(jnp.mean(x * x, axis=-1, keepdims=True) + NORM_EPS) * g


def _rope(x, pos):
    half = HEAD_DIM // 2
    inv = jnp.exp(-math.log(ROPE_THETA) * jnp.arange(half, dtype=F32) * 2.0 / HEAD_DIM)
    ang = pos.astype(F32)[:, None] * inv[None, :]
    cos = jnp.cos(ang)[:, None, :]
    sin = jnp.sin(ang)[:, None, :]
    xs = x.reshape(x.shape[:-1] + (N_HEADS, HEAD_DIM))
    x1, x2 = xs[..., :half], xs[..., half:]
    return jnp.concatenate([x1 * cos - x2 * sin, x2 * cos + x1 * sin], axis=-1).reshape(x.shape)


def kernel(x_prompt, x_sample, state_rwkv_shift, state_rwkv_wkv, cache_k_w128, cache_v_w128, cache_k_w512, cache_v_w512, cache_k_w2048, cache_v_w2048, state_ffn_conv, norm_mix, norm_ffn, norm_final, rwkv_mu, rwkv_w_rkv, rwkv_w0, rwkv_w1, rwkv_w2, rwkv_a0, rwkv_a1, rwkv_a2, rwkv_g1, rwkv_g2, rwkv_k_k, rwkv_k_a, rwkv_r_k, rwkv_ln_w, rwkv_ln_b, rwkv_w_o, attn_w_in, attn_w_o, ffn_w_up, ffn_conv_w, ffn_conv_b, ffn_w_down):
    Bp, Tp, D = x_prompt.shape
    Bs, Ts, _ = x_sample.shape
    Mp, Ms = Bp * Tp, Bs * Ts
    x = jnp.concatenate([x_prompt.reshape(Mp, D), x_sample.reshape(Ms, D)], axis=0)
    split = lambda t: (t[:Mp].reshape(Bp, Tp, -1), t[Mp:].reshape(Bs, Ts, -1))
    join = lambda p, s: jnp.concatenate([p.reshape(Mp, -1), s.reshape(Ms, -1)], axis=0)
    caches_k = (cache_k_w128, cache_k_w512, cache_k_w2048)
    caches_v = (cache_v_w128, cache_v_w512, cache_v_w2048)

    p_shift, p_wkv, s_shift, s_wkv = [], [], [], []
    p_k = [[] for _ in range(N_GROUPS)]
    p_v = [[] for _ in range(N_GROUPS)]
    s_k = [[] for _ in range(N_GROUPS)]
    s_v = [[] for _ in range(N_GROUPS)]
    p_conv, s_conv = [], []

    depth = norm_mix.shape[0]
    for i in range(depth):
        li = i // 2
        h = _rmsnorm(x, norm_mix[i])
        hp, hs = split(h)
        if i % 2 == 0:
            prev = join(jnp.concatenate([jnp.zeros((Bp, 1, D), F32), hp[:, :-1]], axis=1),
                        jnp.concatenate([state_rwkv_shift[li][:, None, :], hs[:, :-1]], axis=1))
            xx = prev - h
            mix = [h + xx * rwkv_mu[li, n] for n in range(6)]
            r = mm(mix[0], rwkv_w_rkv[li, 0])
            k = mm(mix[1], rwkv_w_rkv[li, 1])
            v = mm(mix[2], rwkv_w_rkv[li, 2])
            w_log = -jax.nn.softplus(-(rwkv_w0[li] + mm(jnp.tanh(mm(mix[3], rwkv_w1[li])), rwkv_w2[li]))) - 0.5
            decay = jnp.exp(-jnp.exp(w_log))
            a = jax.nn.sigmoid(rwkv_a0[li] + mm(mm(mix[4], rwkv_a1[li]), rwkv_a2[li]))
            g = mm(jax.nn.sigmoid(mm(mix[5], rwkv_g1[li])), rwkv_g2[li])
            heads = lambda t: t.reshape(-1, N_HEADS, HEAD_DIM)
            kk = heads(k * rwkv_k_k[li])
            kk = (kk / jnp.maximum(jnp.sqrt(jnp.sum(kk * kk, axis=-1, keepdims=True)), 1e-12)).reshape(-1, D)
            k = k * (1.0 + (a - 1.0) * rwkv_k_a[li])
            sa, sb = -kk, kk * a
            yp, stp = _wkv(r[:Mp], decay[:Mp], k[:Mp], v[:Mp], sa[:Mp], sb[:Mp],
                           jnp.zeros((Bp, N_HEADS, HEAD_DIM, HEAD_DIM), F32), Bp, Tp, 32)
            ys, sts = _wkv(r[Mp:], decay[Mp:], k[Mp:], v[Mp:], sa[Mp:], sb[Mp:],
                           state_rwkv_wkv[li], Bs, Ts, Ts)
            y = heads(jnp.concatenate([yp, ys], axis=0))
            mean = jnp.mean(y, axis=-1, keepdims=True)
            var = jnp.mean(jnp.square(y - mean), axis=-1, keepdims=True)
            y = ((y - mean) * lax.rsqrt(var + GN_EPS)).reshape(-1, D) * rwkv_ln_w[li] + rwkv_ln_b[li]
            bonus = jnp.sum(heads(r) * heads(k) * rwkv_r_k[li], axis=-1, keepdims=True) * heads(v)
            y = y + bonus.reshape(-1, D)
            mixed = mm(y * g, rwkv_w_o[li])
            p_shift.append(hp[:, -1])
            s_shift.append(hs[:, -1])
            p_wkv.append(stp)
            s_wkv.append(sts)
        else:
            qkv = mm(h, attn_w_in[li])
            qkv_p, qkv_s = split(qkv)
            col = lambda t, g, n: t[..., (g * 3 + n) * D:(g * 3 + n + 1) * D]
            pos_p = jnp.arange(Tp)
            pos_s = PAST_LEN + jnp.arange(Ts)
            outs, lses = [], []
            for g in range(N_GROUPS):
                q = _rope(col(qkv_p, g, 0), pos_p)
                kr = _rope(col(qkv_p, g, 1), pos_p)
                vv = col(qkv_p, g, 2)
                o, lse = band_attn(q, kr, vv, DILATIONS[g])
                outs.append(o)
                lses.append(lse)
                keep = min(WINDOWS[g], Tp)
                p_k[g].append(kr[:, Tp - keep:].reshape(Bp, keep, N_HEADS, HEAD_DIM))
                p_v[g].append(vv[:, Tp - keep:].reshape(Bp, keep, N_HEADS, HEAD_DIM))
            wts = jax.nn.softmax(jnp.stack(lses, axis=0), axis=0)
            op = sum(jnp.repeat(wts[g], HEAD_DIM, axis=-1) * outs[g] for g in range(N_GROUPS))
            pad = lambda t: jnp.pad(t, ((0, 0), (0, SUBLANES - Ts), (0, 0)))
            qs = jnp.concatenate([_rope(col(qkv_s, g, 0), pos_s) for g in range(N_GROUPS)], axis=-1)
            ks = jnp.concatenate([_rope(col(qkv_s, g, 1), pos_s) for g in range(N_GROUPS)], axis=-1)
            vs = jnp.concatenate([col(qkv_s, g, 2) for g in range(N_GROUPS)], axis=-1)
            osmp = sample_attn(pad(qs), pad(ks), pad(vs),
                               [c[li].reshape(Bs, -1, D) for c in caches_k],
                               [c[li].reshape(Bs, -1, D) for c in caches_v], Ts)
            for g in range(N_GROUPS):
                s_k[g].append(ks[..., g * D:(g + 1) * D].reshape(Bs, Ts, N_HEADS, HEAD_DIM))
                s_v[g].append(vs[..., g * D:(g + 1) * D].reshape(Bs, Ts, N_HEADS, HEAD_DIM))
            mixed = mm(join(op, osmp), attn_w_o[li])
        x = x + mixed

        hn = _rmsnorm(x, norm_ffn[i])
        u = mm(hn, ffn_w_up[i])
        gate_p, gate_s = split(u[:, :D_FF])
        val = u[:, D_FF:]
        ext_p = jnp.concatenate([jnp.zeros((Bp, CONV_W - 1, D_FF), F32), gate_p], axis=1)
        ext_s = jnp.concatenate([state_ffn_conv[i], gate_s], axis=1)
        conv_p = ffn_conv_b[i] + sum(ext_p[:, t:t + Tp] * ffn_conv_w[i, t] for t in range(CONV_W))
        conv_s = ffn_conv_b[i] + sum(ext_s[:, t:t + Ts] * ffn_conv_w[i, t] for t in range(CONV_W))
        act = jax.nn.silu(join(conv_p, conv_s)) * val
        x = x + mm(act, ffn_w_down[i])
        p_conv.append(ext_p[:, -(CONV_W - 1):])
        s_conv.append(ext_s[:, -(CONV_W - 1):])

    y = _rmsnorm(x, norm_final)
    y_prompt, y_sample = split(y)
    st = jnp.stack
    return (y_prompt, y_sample,
            st(p_shift), st(p_wkv),
            st(p_k[0]), st(p_v[0]), st(p_k[1]), st(p_v[1]), st(p_k[2]), st(p_v[2]), st(p_conv),
            st(s_shift), st(s_wkv),
            st(s_k[0]), st(s_v[0]), st(s_k[1]), st(s_v[1]), st(s_k[2]), st(s_v[2]), st(s_conv))
```

```python
import functools
import math

import jax
import jax.numpy as jnp
from jax import lax
from jax.experimental import pallas as pl
from jax.experimental.pallas import tpu as pltpu

F32 = jnp.float32
BF16 = jnp.bfloat16

D_MODEL = 1024
HEAD_DIM = 64
N_HEADS = 16
D_FF = 2816
CONV_W = 3
N_GROUPS = 3
WINDOWS = (128, 512, 2048)
DILATIONS = (1, 4, 16)
BAND = 128
GN_EPS = 64e-5
NORM_EPS = 1e-6
ROPE_THETA = 10000.0
PAST_LEN = 2048
LANES = 128
SUBLANES = 8
VMEM_LIMIT = 56 * 1024 * 1024


def _params(*sem):
    return pltpu.CompilerParams(dimension_semantics=sem, vmem_limit_bytes=VMEM_LIMIT)


def _mm_kernel(x_ref, w_ref, o_ref):
    o_ref[...] = jnp.dot(x_ref[...].astype(BF16), w_ref[...],
                         preferred_element_type=F32).astype(o_ref.dtype)


def mm(x, w, tm=1024, tn=1024):
    M, K = x.shape
    N = w.shape[1]
    tm = next(t for t in (tm, tm // 2, M) if M % t == 0)
    tn = next(t for t in (tn, tn // 2, N) if N % t == 0)
    return pl.pallas_call(
        _mm_kernel,
        grid=(M // tm, N // tn),
        in_specs=[pl.BlockSpec((tm, K), lambda i, j: (i, 0)),
                  pl.BlockSpec((K, tn), lambda i, j: (0, j))],
        out_specs=pl.BlockSpec((tm, tn), lambda i, j: (i, j)),
        out_shape=jax.ShapeDtypeStruct((M, N), F32),
        compiler_params=_params("parallel", "parallel"),
        name="mm",
    )(x, w.astype(BF16))


def _wkv_kernel(a_ref, wr_ref, w_ref, b_ref, k_ref, v_ref, br_ref, kr_ref, s0_ref,
                y_ref, st_ref, state, *, steps):
    @pl.when(pl.program_id(1) == 0)
    def _():
        state[...] = s0_ref[...]

    def row(ref, t, k):
        return jnp.broadcast_to(ref[t, pl.ds(k, 1), :], (SUBLANES, LANES))[None]

    def step(t, carry):
        sa = jnp.zeros((4, SUBLANES, LANES), F32)
        yy = jnp.zeros((4, SUBLANES, LANES), F32)
        for k in range(HEAD_DIM):
            sk = state[k]
            sa = sa + sk * row(a_ref, t, k)
            yy = yy + sk * row(wr_ref, t, k)
        v = v_ref[t]
        y_ref[t] = yy + sa * row(br_ref, t, 0) + v * row(kr_ref, t, 0)
        for k in range(HEAD_DIM):
            state[k] = state[k] * row(w_ref, t, k) + sa * row(b_ref, t, k) + v * row(k_ref, t, k)
        return carry

    lax.fori_loop(0, steps, step, 0)

    @pl.when(pl.program_id(1) == pl.num_programs(1) - 1)
    def _():
        st_ref[...] = state[...]


def wkv_scan(rows, v, br, kr, s0, steps):
    T, _, L = rows[0].shape
    nl = L // LANES
    assert T % steps == 0
    row_spec = pl.BlockSpec((steps, HEAD_DIM, LANES), lambda l, t: (t, 0, l))
    col_spec = pl.BlockSpec((steps, 4, SUBLANES, LANES), lambda l, t: (t, 0, 0, l))
    sc_spec = pl.BlockSpec((steps, 1, LANES), lambda l, t: (t, 0, l))
    st_spec = pl.BlockSpec((HEAD_DIM, 4, SUBLANES, LANES), lambda l, t: (0, 0, 0, l))
    return pl.pallas_call(
        functools.partial(_wkv_kernel, steps=steps),
        grid=(nl, T // steps),
        in_specs=[row_spec] * 5 + [col_spec, sc_spec, sc_spec, st_spec],
        out_specs=[col_spec, st_spec],
        out_shape=[jax.ShapeDtypeStruct(v.shape, F32), jax.ShapeDtypeStruct(s0.shape, F32)],
        scratch_shapes=[pltpu.VMEM((HEAD_DIM, 4, SUBLANES, LANES), F32)],
        compiler_params=_params("parallel", "arbitrary"),
        name="wkv_scan",
    )(*rows, v, br, kr, s0)


def _to_rows(x, B, T):
    nl = B * N_HEADS // 64
    x = x.reshape(B, T, N_HEADS, HEAD_DIM).transpose(1, 3, 0, 2).reshape(T, HEAD_DIM, nl, 1, 64)
    return jnp.broadcast_to(x, (T, HEAD_DIM, nl, 2, 64)).reshape(T, HEAD_DIM, nl * LANES)


def _to_cols(x, B, T):
    nl = B * N_HEADS // 64
    x = x.reshape(B, T, N_HEADS, 2, 32).transpose(1, 4, 0, 2, 3).reshape(T, 32, nl, 64, 2)
    return x.transpose(0, 1, 2, 4, 3).reshape(T, 4, SUBLANES, nl * LANES)


def _from_cols(y, B, T):
    nl = B * N_HEADS // 64
    y = y.reshape(T, 32, nl, 2, 64).transpose(0, 2, 4, 3, 1)
    return y.reshape(T, B, N_HEADS, HEAD_DIM).transpose(1, 0, 2, 3).reshape(B * T, D_MODEL)


def _head_scalar(x, B, T):
    nl = B * N_HEADS // 64
    x = x.reshape(B, T, N_HEADS).transpose(1, 0, 2).reshape(T, nl, 1, 64)
    return jnp.broadcast_to(x, (T, nl, 2, 64)).reshape(T, 1, nl * LANES)


def _state_in(s, B):
    nl = B * N_HEADS // 64
    s = s.reshape(B, N_HEADS, 2, 32, HEAD_DIM).transpose(4, 3, 0, 1, 2).reshape(HEAD_DIM, 32, nl, 64, 2)
    return s.transpose(0, 1, 2, 4, 3).reshape(HEAD_DIM, 4, SUBLANES, nl * LANES)


def _state_out(s, B):
    nl = B * N_HEADS // 64
    s = s.reshape(HEAD_DIM, 32, nl, 2, 64).transpose(2, 4, 3, 1, 0)
    return s.reshape(B, N_HEADS, HEAD_DIM, HEAD_DIM)


def _wkv(r, decay, k, v, a, b, s0, B, T, steps):
    hsum = lambda x: x.reshape(B * T, N_HEADS, HEAD_DIM).sum(-1)
    rows = [_to_rows(x, B, T) for x in (a, decay * r, decay, b, k)]
    y, st = wkv_scan(rows, _to_cols(v, B, T), _head_scalar(hsum(b * r), B, T),
                     _head_scalar(hsum(k * r), B, T), _state_in(s0, B), steps)
    return _from_cols(y, B, T), _state_out(st, B)


def _rope_rows(x, cos, sin):
    reps = x.shape[1] // LANES
    lane = lax.broadcasted_iota(jnp.int32, x.shape, 1)
    partner = jnp.where(lane % HEAD_DIM < HEAD_DIM // 2,
                        pltpu.roll(x, x.shape[1] - HEAD_DIM // 2, axis=1),
                        pltpu.roll(x, HEAD_DIM // 2, axis=1))
    return x * jnp.tile(cos, (1, reps)) + partner * jnp.tile(sin, (1, reps))


def _band_attn_kernel(q_ref, kc_ref, kp_ref, vc_ref, vp_ref, cc_ref, sc_ref, cp_ref, sp_ref,
                      o_ref, lse_ref, kr_ref):
    n = pl.program_id(2)
    i = lax.broadcasted_iota(jnp.int32, (BAND, 2 * BAND), 0)
    j = lax.broadcasted_iota(jnp.int32, (BAND, 2 * BAND), 1)
    valid = (j >= i) & (j <= i + BAND) & ((n > 0) | (j >= BAND))
    q_all = _rope_rows(q_ref[0], cc_ref[...], sc_ref[...])
    kc_all = _rope_rows(kc_ref[0], cc_ref[...], sc_ref[...])
    kp_all = _rope_rows(kp_ref[0], cp_ref[...], sp_ref[...])
    kr_ref[0] = kc_all
    lses = []
    for h in range(N_HEADS):
        sl = slice(h * HEAD_DIM, (h + 1) * HEAD_DIM)
        q = q_all[:, sl].astype(BF16)
        k = jnp.concatenate([kp_all[:, sl], kc_all[:, sl]], axis=0).astype(BF16)
        v = jnp.concatenate([vp_ref[0, :, sl], vc_ref[0, :, sl]], axis=0).astype(BF16)
        s = lax.dot_general(q, k, (((1,), (1,)), ((), ())), preferred_element_type=F32)
        s = jnp.where(valid, s * (HEAD_DIM ** -0.5), -jnp.inf)
        m = jnp.max(s, axis=-1, keepdims=True)
        p = jnp.exp(s - m)
        den = jnp.sum(p, axis=-1, keepdims=True)
        o = jnp.dot(p.astype(BF16), v, preferred_element_type=F32)
        o_ref[0, :, sl] = o / den
        lses.append(m + jnp.log(den))
    lse_ref[0, 0] = jnp.concatenate(lses, axis=1)


def band_attn(qkv, g, cos, sin):
    B, T, W = qkv.shape
    D = D_MODEL
    dil = DILATIONS[g]
    L = T // dil
    nb = L // BAND
    ncol = W // D
    x = qkv.reshape(B, L, dil * W)
    tab = lambda t: t.reshape(L, dil * LANES)
    cur = lambda m: pl.BlockSpec((1, BAND, D), lambda b, c, n: (b, n, c * ncol + g * 3 + m))
    prev = lambda m: pl.BlockSpec((1, BAND, D), lambda b, c, n: (b, jnp.maximum(n - 1, 0), c * ncol + g * 3 + m))
    tcur = pl.BlockSpec((BAND, LANES), lambda b, c, n: (n, c))
    tprev = pl.BlockSpec((BAND, LANES), lambda b, c, n: (jnp.maximum(n - 1, 0), c))
    out = pl.BlockSpec((1, BAND, D), lambda b, c, n: (b, n, c))
    o, lse, kr = pl.pallas_call(
        _band_attn_kernel,
        grid=(B, dil, nb),
        in_specs=[cur(0), cur(1), prev(1), cur(2), prev(2), tcur, tcur, tprev, tprev],
        out_specs=[out, pl.BlockSpec((1, 1, BAND, N_HEADS), lambda b, c, n: (b, c, n, 0)), out],
        out_shape=[jax.ShapeDtypeStruct((B, L, dil * D), F32),
                   jax.ShapeDtypeStruct((B, dil, L, N_HEADS), F32),
                   jax.ShapeDtypeStruct((B, L, dil * D), F32)],
        compiler_params=_params("parallel", "parallel", "arbitrary"),
        name="band_attn",
    )(x, x, x, x, x, tab(cos), tab(sin), tab(cos), tab(sin))
    return o.reshape(B, T, D), lse.transpose(0, 2, 1, 3).reshape(B, T, N_HEADS), kr.reshape(B, T, D)


def _sample_attn_kernel(q_ref, kn_ref, vn_ref, ck0_ref, cv0_ref, ck1_ref, cv1_ref, ck2_ref, cv2_ref,
                        o_ref, *, n_new):
    ones = jnp.ones((HEAD_DIM, HEAD_DIM), BF16)
    scale = HEAD_DIM ** -0.5
    crow = lax.broadcasted_iota(jnp.int32, (BAND, N_HEADS, HEAD_DIM), 0)
    nrow = lax.broadcasted_iota(jnp.int32, (n_new, N_HEADS, HEAD_DIM), 0)
    caches = ((ck0_ref, cv0_ref), (ck1_ref, cv1_ref), (ck2_ref, cv2_ref))

    def scores(k, q):
        rows = k.shape[0]
        prod = (k * q[None]).reshape(rows * N_HEADS, HEAD_DIM).astype(BF16)
        return jnp.dot(prod, ones, preferred_element_type=F32).reshape(rows, N_HEADS, HEAD_DIM) * scale

    for j in range(n_new):
        ms, dens, accs = [], [], []
        for g in range(N_GROUPS):
            q = q_ref[0, j, g]
            ck_ref, cv_ref = caches[g]
            if g == 0:
                kc, vc = ck_ref[0, 0], cv_ref[0, 0]
                cmask = crow >= j
                nmask = nrow <= j
            else:
                kc, vc = ck_ref[0, 0, :, j], cv_ref[0, 0, :, j]
                cmask = None
                nmask = nrow == j
            kn, vn = kn_ref[0, :, g], vn_ref[0, :, g]
            sc = scores(kc, q)
            if cmask is not None:
                sc = jnp.where(cmask, sc, -jnp.inf)
            sn = jnp.where(nmask, scores(kn, q), -jnp.inf)
            m = jnp.maximum(jnp.max(sc, axis=0), jnp.max(sn, axis=0))
            pc = jnp.exp(sc - m[None])
            pn = jnp.exp(sn - m[None])
            ms.append(m)
            dens.append(jnp.sum(pc, axis=0) + jnp.sum(pn, axis=0))
            accs.append(jnp.sum(pc * vc, axis=0) + jnp.sum(pn * vn, axis=0))
        mx = jnp.maximum(jnp.maximum(ms[0], ms[1]), ms[2])
        fs = [jnp.exp(m - mx) for m in ms]
        tot = fs[0] * dens[0] + fs[1] * dens[1] + fs[2] * dens[2]
        o_ref[0, j] = (fs[0] * accs[0] + fs[1] * accs[1] + fs[2] * accs[2]) / tot


def sample_attn(q, kn, vn, caches_k, caches_v):
    B, n_new = q.shape[:2]
    new_spec = pl.BlockSpec((1, n_new, N_GROUPS, N_HEADS, HEAD_DIM), lambda b: (b, 0, 0, 0, 0))
    specs = [new_spec] * 3
    args = [q, kn, vn]
    for g in range(N_GROUPS):
        dil = DILATIONS[g]
        assert caches_k[g].shape[2] == BAND * dil and (g == 0 or n_new <= dil)
        if g == 0:
            shape = (1, B, BAND, N_HEADS, HEAD_DIM)
            spec = pl.BlockSpec((1, 1, BAND, N_HEADS, HEAD_DIM), lambda b: (0, b, 0, 0, 0))
        else:
            shape = (1, B, BAND, dil, N_HEADS, HEAD_DIM)
            spec = pl.BlockSpec((1, 1, BAND, n_new, N_HEADS, HEAD_DIM), lambda b: (0, b, 0, 0, 0, 0))
        specs += [spec, spec]
        args += [caches_k[g].reshape(shape), caches_v[g].reshape(shape)]
    return pl.pallas_call(
        functools.partial(_sample_attn_kernel, n_new=n_new),
        grid=(B,),
        in_specs=specs,
        out_specs=pl.BlockSpec((1, n_new, N_HEADS, HEAD_DIM), lambda b: (b, 0, 0, 0)),
        out_shape=jax.ShapeDtypeStruct((B, n_new, N_HEADS, HEAD_DIM), F32),
        compiler_params=_params("parallel"),
        name="sample_attn",
    )(*args)


def _ffn_kernel(xp_ref, xs_ref, halo_ref, g_ref, wg_ref, wv_ref, cw_ref, cb_ref, wd_ref, e1_ref, e2_ref, gf_ref,
                op_ref, os_ref, hn_scr, hh_scr, acc_scr, *, seq_tiles, sample_tile, seq_new, final_norm):
    i = pl.program_id(0)
    j = pl.program_id(1)
    is_sample = i == sample_tile

    def norm(x):
        return (x * lax.rsqrt(jnp.mean(x * x, axis=-1, keepdims=True) + NORM_EPS) * g_ref[...]).astype(BF16)

    def x_tile():
        return jnp.where(is_sample, xs_ref[...], xp_ref[...])

    @pl.when(j == 0)
    def _():
        hn_scr[...] = norm(x_tile())
        hh_scr[...] = norm(halo_ref[...]).astype(F32)
        acc_scr[...] = jnp.zeros_like(acc_scr)

    hn = hn_scr[...]
    gate = jnp.dot(hn, wg_ref[...], preferred_element_type=F32)
    val = jnp.dot(hn, wv_ref[...], preferred_element_type=F32)
    gh = jnp.dot(hh_scr[...].astype(BF16), wg_ref[...], preferred_element_type=F32)
    gh = jnp.where(i % seq_tiles == 0, 0.0, gh)
    row = lax.broadcasted_iota(jnp.int32, gate.shape, 0)
    roll1 = pltpu.roll(gate, 1, axis=0)
    roll2 = pltpu.roll(gate, 2, axis=0)
    pos = row % seq_new
    prev1 = jnp.where(is_sample,
                      jnp.where(pos == 0, e1_ref[...], roll1),
                      jnp.where(row == 0, gh[7:8], roll1))
    prev2 = jnp.where(is_sample,
                      jnp.where(pos < 2, e2_ref[...], roll2),
                      jnp.where(row == 0, gh[6:7], jnp.where(row == 1, gh[7:8], roll2)))
    conv = cb_ref[...] + prev2 * cw_ref[0:1] + prev1 * cw_ref[1:2] + gate * cw_ref[2:3]
    act = (conv * jax.nn.sigmoid(conv) * val).astype(BF16)
    acc_scr[...] += jnp.dot(act, wd_ref[...], preferred_element_type=F32)

    @pl.when(j == pl.num_programs(1) - 1)
    def _():
        y = x_tile() + acc_scr[...]
        if final_norm:
            y = y * lax.rsqrt(jnp.mean(y * y, axis=-1, keepdims=True) + NORM_EPS) * gf_ref[...]

        @pl.when(is_sample)
        def _():
            os_ref[...] = y

        @pl.when(jnp.logical_not(is_sample))
        def _():
            op_ref[...] = y


def conv_glu_ffn(xp, xs, g, w_up, conv_w, conv_b, w_down, e1, e2, g_final, seq_len, seq_new, tm=512, tf=256):
    Mp, D = xp.shape
    F = w_down.shape[0]
    assert Mp % tm == 0 and F % tf == 0 and seq_len % tm == 0 and tm % seq_new == 0
    assert xs.shape == (tm, D) and e1.shape == (tm, F) and e2.shape == (tm, F)
    nf = F // tf
    sample_tile = Mp // tm
    prow = pl.BlockSpec((tm, D), lambda i, j: (jnp.minimum(i, sample_tile - 1), 0))
    srow = pl.BlockSpec((tm, D), lambda i, j: (0, 0))
    halo = pl.BlockSpec((SUBLANES, D),
                        lambda i, j: (jnp.clip(i * (tm // SUBLANES) - 1, 0, Mp // SUBLANES - 1), 0))
    vec = pl.BlockSpec((1, D), lambda i, j: (0, 0))
    fill = pl.BlockSpec((tm, tf), lambda i, j: (0, jnp.where(i == sample_tile, j, 0)))
    w_up = w_up.astype(BF16)
    return pl.pallas_call(
        functools.partial(_ffn_kernel, seq_tiles=seq_len // tm, sample_tile=sample_tile, seq_new=seq_new,
                          final_norm=g_final is not None),
        grid=(sample_tile + 1, nf),
        in_specs=[prow, srow, halo, vec,
                  pl.BlockSpec((D, tf), lambda i, j: (0, j)),
                  pl.BlockSpec((D, tf), lambda i, j: (0, j + nf)),
                  pl.BlockSpec((CONV_W, tf), lambda i, j: (0, j)),
                  pl.BlockSpec((1, tf), lambda i, j: (0, j)),
                  pl.BlockSpec((tf, D), lambda i, j: (j, 0)),
                  fill, fill, vec],
        out_specs=[prow, srow],
        out_shape=[jax.ShapeDtypeStruct((Mp, D), F32), jax.ShapeDtypeStruct((tm, D), F32)],
        scratch_shapes=[pltpu.VMEM((tm, D), BF16), pltpu.VMEM((SUBLANES, D), F32), pltpu.VMEM((tm, D), F32)],
        compiler_params=_params("arbitrary", "arbitrary"),
        name="conv_glu_ffn",
    )(xp, xs, xp, g.reshape(1, D), w_up, w_up, conv_w, conv_b.reshape(1, F), w_down.astype(BF16), e1, e2,
      (g if g_final is None else g_final).reshape(1, D))


def _rmsnorm(x, g):
    return x * lax.rsqrt(jnp.mean(x * x, axis=-1, keepdims=True) + NORM_EPS) * g


def _rope_angles(pos):
    half = HEAD_DIM // 2
    inv = jnp.exp(-math.log(ROPE_THETA) * jnp.arange(half, dtype=F32) * 2.0 / HEAD_DIM)
    return pos.astype(F32)[:, None] * inv[None, :]


def _rope_tables(pos):
    ang = _rope_angles(pos)
    cos, sin = jnp.cos(ang), jnp.sin(ang)
    return jnp.tile(cos, (1, LANES // cos.shape[1])), jnp.tile(jnp.concatenate([-sin, sin], axis=1), (1, 2))


def _rope(x, pos):
    half = HEAD_DIM // 2
    ang = _rope_angles(pos)
    cos = jnp.cos(ang)[:, None, None, :]
    sin = jnp.sin(ang)[:, None, None, :]
    x1, x2 = x[..., :half], x[..., half:]
    return jnp.concatenate([x1 * cos - x2 * sin, x2 * cos + x1 * sin], axis=-1)


def kernel(x_prompt, x_sample, state_rwkv_shift, state_rwkv_wkv, cache_k_w128, cache_v_w128, cache_k_w512, cache_v_w512, cache_k_w2048, cache_v_w2048, state_ffn_conv, norm_mix, norm_ffn, norm_final, rwkv_mu, rwkv_w_rkv, rwkv_w0, rwkv_w1, rwkv_w2, rwkv_a0, rwkv_a1, rwkv_a2, rwkv_g1, rwkv_g2, rwkv_k_k, rwkv_k_a, rwkv_r_k, rwkv_ln_w, rwkv_ln_b, rwkv_w_o, attn_w_in, attn_w_o, ffn_w_up, ffn_conv_w, ffn_conv_b, ffn_w_down):
    Bp, Tp, D = x_prompt.shape
    Bs, Ts, _ = x_sample.shape
    Mp, Ms = Bp * Tp, Bs * Ts
    assert Ts >= CONV_W - 1 and Tp >= CONV_W - 1 and CONV_W == 3
    xp = x_prompt.reshape(Mp, D)
    xs = x_sample.reshape(Ms, D)
    caches_k = (cache_k_w128, cache_k_w512, cache_k_w2048)
    caches_v = (cache_v_w128, cache_v_w512, cache_v_w2048)

    p_shift, p_wkv, s_shift, s_wkv = [], [], [], []
    p_k = [[] for _ in range(N_GROUPS)]
    p_v = [[] for _ in range(N_GROUPS)]
    s_k = [[] for _ in range(N_GROUPS)]
    s_v = [[] for _ in range(N_GROUPS)]
    p_conv, s_conv = [], []

    def time_mix(li, h, x_last, s0, B, T, steps):
        h3 = h.reshape(B, T, D)
        prev = jnp.concatenate([x_last[:, None, :], h3[:, :-1]], axis=1).reshape(B * T, D)
        xx = prev - h
        mix = [(h + xx * rwkv_mu[li, n]).astype(BF16) for n in range(6)]
        r = mm(mix[0], rwkv_w_rkv[li, 0])
        k = mm(mix[1], rwkv_w_rkv[li, 1])
        v = mm(mix[2], rwkv_w_rkv[li, 2])
        w_log = -jax.nn.softplus(-(rwkv_w0[li] + mm(jnp.tanh(mm(mix[3], rwkv_w1[li])), rwkv_w2[li]))) - 0.5
        decay = jnp.exp(-jnp.exp(w_log))
        a = jax.nn.sigmoid(rwkv_a0[li] + mm(mm(mix[4], rwkv_a1[li]), rwkv_a2[li]))
        g = mm(jax.nn.sigmoid(mm(mix[5], rwkv_g1[li])), rwkv_g2[li])
        heads = lambda t: t.reshape(-1, N_HEADS, HEAD_DIM)
        kk = heads(k * rwkv_k_k[li])
        kk = (kk / jnp.maximum(jnp.sqrt(jnp.sum(kk * kk, axis=-1, keepdims=True)), 1e-12)).reshape(-1, D)
        k = k * (1.0 + (a - 1.0) * rwkv_k_a[li])
        y, st = _wkv(r, decay, k, v, -kk, kk * a, s0, B, T, steps)
        y = heads(y)
        mean = jnp.mean(y, axis=-1, keepdims=True)
        var = jnp.mean(jnp.square(y - mean), axis=-1, keepdims=True)
        y = ((y - mean) * lax.rsqrt(var + GN_EPS)).reshape(-1, D) * rwkv_ln_w[li] + rwkv_ln_b[li]
        bonus = jnp.sum(heads(r) * heads(k) * rwkv_r_k[li], axis=-1, keepdims=True) * heads(v)
        y = y + bonus.reshape(-1, D)
        return mm((y * g).astype(BF16), rwkv_w_o[li]), st

    depth = norm_mix.shape[0]
    for i in range(depth):
        li = i // 2
        hp = _rmsnorm(xp, norm_mix[i])
        hs = _rmsnorm(xs, norm_mix[i])
        if i % 2 == 0:
            mp, stp = time_mix(li, hp, jnp.zeros((Bp, D), F32),
                               jnp.zeros((Bp, N_HEADS, HEAD_DIM, HEAD_DIM), F32), Bp, Tp, 32)
            ms, sts = time_mix(li, hs, state_rwkv_shift[li], state_rwkv_wkv[li], Bs, Ts, Ts)
            p_shift.append(hp.reshape(Bp, Tp, D)[:, -1])
            s_shift.append(hs.reshape(Bs, Ts, D)[:, -1])
            p_wkv.append(stp)
            s_wkv.append(sts)
        else:
            assert len(caches_k[0].shape) == 5 and caches_k[0].shape[0] == 1 and li == 0
            qkv_p = mm(hp.astype(BF16), attn_w_in[li]).reshape(Bp, Tp, -1)
            qkv_s = mm(hs.astype(BF16), attn_w_in[li]).reshape(Bs, Ts, N_GROUPS, 3, N_HEADS, HEAD_DIM)
            cos_p, sin_p = _rope_tables(jnp.arange(Tp))
            outs, lses = [], []
            for g in range(N_GROUPS):
                o, lse, kr = band_attn(qkv_p, g, cos_p, sin_p)
                outs.append(o)
                lses.append(lse)
                keep = min(WINDOWS[g], Tp)
                vv = qkv_p[:, Tp - keep:, (g * 3 + 2) * D:(g * 3 + 3) * D]
                p_k[g].append(kr[:, Tp - keep:].reshape(Bp, keep, N_HEADS, HEAD_DIM))
                p_v[g].append(vv.reshape(Bp, keep, N_HEADS, HEAD_DIM))
            wts = jax.nn.softmax(jnp.stack(lses, axis=0), axis=0)
            op = sum(jnp.repeat(wts[g], HEAD_DIM, axis=-1) * outs[g] for g in range(N_GROUPS))
            pos_s = PAST_LEN + jnp.arange(Ts)
            q_s = _rope(qkv_s[:, :, :, 0], pos_s)
            k_s = _rope(qkv_s[:, :, :, 1], pos_s)
            v_s = qkv_s[:, :, :, 2]
            osmp = sample_attn(q_s, k_s, v_s, caches_k, caches_v)
            for g in range(N_GROUPS):
                s_k[g].append(k_s[:, :, g])
                s_v[g].append(v_s[:, :, g])
            mp = mm(op.reshape(Mp, D).astype(BF16), attn_w_o[li])
            ms = mm(osmp.reshape(Ms, D), attn_w_o[li])
        xp = xp + mp
        xs = xs + ms

        buf = state_ffn_conv[i]
        first = (jnp.arange(Ms) % Ts == 0)[:, None]
        e1 = jnp.repeat(buf[:, 1], Ts, axis=0)
        e2 = jnp.where(first, jnp.repeat(buf[:, 0], Ts, axis=0), e1)
        tail = jnp.concatenate([xp.reshape(Bp, Tp, D)[:, Tp - 2:].reshape(Bp * 2, D),
                                xs.reshape(Bs, Ts, D)[:, Ts - 2:].reshape(Bs * 2, D)], axis=0)
        gate_tail = mm(_rmsnorm(tail, norm_ffn[i]), ffn_w_up[i][:, :D_FF])
        p_conv.append(gate_tail[:Bp * 2].reshape(Bp, 2, D_FF))
        s_conv.append(gate_tail[Bp * 2:].reshape(Bs, 2, D_FF))
        xp, xs = conv_glu_ffn(xp, xs, norm_ffn[i], ffn_w_up[i], ffn_conv_w[i], ffn_conv_b[i], ffn_w_down[i],
                              e1, e2, norm_final if i == depth - 1 else None, Tp, Ts)

    y_prompt = xp.reshape(Bp, Tp, D)
    y_sample = xs.reshape(Bs, Ts, D)
    st = jnp.stack
    return (y_prompt, y_sample,
            st(p_shift), st(p_wkv),
            st(p_k[0]), st(p_v[0]), st(p_k[1]), st(p_v[1]), st(p_k[2]), st(p_v[2]), st(p_conv),
            st(s_shift), st(s_wkv),
            st(s_k[0]), st(s_v[0]), st(s_k[1]), st(s_v[1]), st(s_k[2]), st(s_v[2]), st(s_conv))
```

```python
import functools
import math

import jax
import jax.numpy as jnp
from jax import lax
from jax.experimental import pallas as pl
from jax.experimental.pallas import tpu as pltpu

F32 = jnp.float32
BF16 = jnp.bfloat16

D_MODEL = 1024
HEAD_DIM = 64
N_HEADS = 16
D_FF = 2816
CONV_W = 3
N_GROUPS = 3
WINDOWS = (128, 512, 2048)
DILATIONS = (1, 4, 16)
BAND = 128
GN_EPS = 64e-5
NORM_EPS = 1e-6
ROPE_THETA = 10000.0
PAST_LEN = 2048
LANES = 128
SUBLANES = 8
VMEM_LIMIT = 56 * 1024 * 1024


def _params(*sem):
    return pltpu.CompilerParams(dimension_semantics=sem, vmem_limit_bytes=VMEM_LIMIT)


def _mm_kernel(x_ref, w_ref, o_ref):
    o_ref[...] = jnp.dot(x_ref[...].astype(BF16), w_ref[...],
                         preferred_element_type=F32).astype(o_ref.dtype)


def mm(x, w, tm=1024, tn=1024):
    M, K = x.shape
    N = w.shape[1]
    tm = next(t for t in (tm, tm // 2, M) if M % t == 0)
    tn = next(t for t in (tn, tn // 2, N) if N % t == 0)
    return pl.pallas_call(
        _mm_kernel,
        grid=(M // tm, N // tn),
        in_specs=[pl.BlockSpec((tm, K), lambda i, j: (i, 0)),
                  pl.BlockSpec((K, tn), lambda i, j: (0, j))],
        out_specs=pl.BlockSpec((tm, tn), lambda i, j: (i, j)),
        out_shape=jax.ShapeDtypeStruct((M, N), F32),
        compiler_params=_params("parallel", "parallel"),
        name="mm",
    )(x, w.astype(BF16))


def _wkv_kernel(a_ref, wr_ref, w_ref, b_ref, k_ref, v_ref, br_ref, kr_ref, s0_ref,
                y_ref, st_ref, state, *, steps):
    @pl.when(pl.program_id(1) == 0)
    def _():
        state[...] = s0_ref[...]

    def row(ref, t, k):
        return jnp.broadcast_to(ref[t, pl.ds(k, 1), :], (SUBLANES, LANES))[None]

    def step(t, carry):
        sa = jnp.zeros((4, SUBLANES, LANES), F32)
        yy = jnp.zeros((4, SUBLANES, LANES), F32)
        for k in range(HEAD_DIM):
            sk = state[k]
            sa = sa + sk * row(a_ref, t, k)
            yy = yy + sk * row(wr_ref, t, k)
        v = v_ref[t]
        y_ref[t] = yy + sa * row(br_ref, t, 0) + v * row(kr_ref, t, 0)
        for k in range(HEAD_DIM):
            state[k] = state[k] * row(w_ref, t, k) + sa * row(b_ref, t, k) + v * row(k_ref, t, k)
        return carry

    lax.fori_loop(0, steps, step, 0)

    @pl.when(pl.program_id(1) == pl.num_programs(1) - 1)
    def _():
        st_ref[...] = state[...]


def wkv_scan(rows, v, br, kr, s0, steps):
    T, _, L = rows[0].shape
    nl = L // LANES
    assert T % steps == 0
    row_spec = pl.BlockSpec((steps, HEAD_DIM, LANES), lambda l, t: (t, 0, l))
    col_spec = pl.BlockSpec((steps, 4, SUBLANES, LANES), lambda l, t: (t, 0, 0, l))
    sc_spec = pl.BlockSpec((steps, 1, LANES), lambda l, t: (t, 0, l))
    st_spec = pl.BlockSpec((HEAD_DIM, 4, SUBLANES, LANES), lambda l, t: (0, 0, 0, l))
    return pl.pallas_call(
        functools.partial(_wkv_kernel, steps=steps),
        grid=(nl, T // steps),
        in_specs=[row_spec] * 5 + [col_spec, sc_spec, sc_spec, st_spec],
        out_specs=[col_spec, st_spec],
        out_shape=[jax.ShapeDtypeStruct(v.shape, F32), jax.ShapeDtypeStruct(s0.shape, F32)],
        scratch_shapes=[pltpu.VMEM((HEAD_DIM, 4, SUBLANES, LANES), F32)],
        compiler_params=_params("parallel", "arbitrary"),
        name="wkv_scan",
    )(*rows, v, br, kr, s0)


def _to_rows(x, B, T):
    nl = B * N_HEADS // 64
    x = x.reshape(B, T, N_HEADS, HEAD_DIM).transpose(1, 3, 0, 2).reshape(T, HEAD_DIM, nl, 1, 64)
    return jnp.broadcast_to(x, (T, HEAD_DIM, nl, 2, 64)).reshape(T, HEAD_DIM, nl * LANES)


def _to_cols(x, B, T):
    nl = B * N_HEADS // 64
    x = x.reshape(B, T, N_HEADS, 2, 32).transpose(1, 4, 0, 2, 3).reshape(T, 32, nl, 64, 2)
    return x.transpose(0, 1, 2, 4, 3).reshape(T, 4, SUBLANES, nl * LANES)


def _from_cols(y, B, T):
    nl = B * N_HEADS // 64
    y = y.reshape(T, 32, nl, 2, 64).transpose(0, 2, 4, 3, 1)
    return y.reshape(T, B, N_HEADS, HEAD_DIM).transpose(1, 0, 2, 3).reshape(B * T, D_MODEL)


def _head_scalar(x, B, T):
    nl = B * N_HEADS // 64
    x = x.reshape(B, T, N_HEADS).transpose(1, 0, 2).reshape(T, nl, 1, 64)
    return jnp.broadcast_to(x, (T, nl, 2, 64)).reshape(T, 1, nl * LANES)


def _state_in(s, B):
    nl = B * N_HEADS // 64
    s = s.reshape(B, N_HEADS, 2, 32, HEAD_DIM).transpose(4, 3, 0, 1, 2).reshape(HEAD_DIM, 32, nl, 64, 2)
    return s.transpose(0, 1, 2, 4, 3).reshape(HEAD_DIM, 4, SUBLANES, nl * LANES)


def _state_out(s, B):
    nl = B * N_HEADS // 64
    s = s.reshape(HEAD_DIM, 32, nl, 2, 64).transpose(2, 4, 3, 1, 0)
    return s.reshape(B, N_HEADS, HEAD_DIM, HEAD_DIM)


def _wkv(r, decay, k, v, a, b, s0, B, T, steps):
    hsum = lambda x: x.reshape(B * T, N_HEADS, HEAD_DIM).sum(-1)
    rows = [_to_rows(x, B, T) for x in (a, decay * r, decay, b, k)]
    y, st = wkv_scan(rows, _to_cols(v, B, T), _head_scalar(hsum(b * r), B, T),
                     _head_scalar(hsum(k * r), B, T), _state_in(s0, B), steps)
    return _from_cols(y, B, T), _state_out(st, B)


def _band_attn_kernel(q_ref, k_ref, v_ref, cos_ref, sin_ref, o_ref, lse_ref, kr_ref, kprev, vprev):
    n = pl.program_id(1)

    @pl.when(n == 0)
    def _():
        kprev[...] = jnp.zeros_like(kprev)
        vprev[...] = jnp.zeros_like(vprev)

    i = lax.broadcasted_iota(jnp.int32, (BAND, 2 * BAND), 0)
    j = lax.broadcasted_iota(jnp.int32, (BAND, 2 * BAND), 1)
    valid = (j >= i) & (j <= i + BAND) & ((n > 0) | (j >= BAND))
    lane = lax.broadcasted_iota(jnp.int32, (BAND, LANES), 1)
    first_head = lane < HEAD_DIM
    first_half = lane % HEAD_DIM < HEAD_DIM // 2
    cos, sin = cos_ref[...], sin_ref[...]

    def rope(x):
        partner = jnp.where(first_half, pltpu.roll(x, LANES - HEAD_DIM // 2, axis=1),
                            pltpu.roll(x, HEAD_DIM // 2, axis=1))
        return x * cos + partner * sin

    for p in range(N_HEADS // 2):
        cs = slice(p * LANES, (p + 1) * LANES)
        q = rope(q_ref[0, :, cs])
        k = rope(k_ref[0, :, cs])
        kr_ref[0, :, cs] = k
        kb = k.astype(BF16)
        vb = v_ref[0, :, cs].astype(BF16)
        kcat = jnp.concatenate([kprev[:, cs], kb], axis=0)
        vcat = jnp.concatenate([vprev[:, cs], vb], axis=0)
        kprev[:, cs] = kb
        vprev[:, cs] = vb
        outs, lses = [], []
        for head_mask in (first_head, jnp.logical_not(first_head)):
            qm = jnp.where(head_mask, q, 0.0).astype(BF16)
            s = lax.dot_general(qm, kcat, (((1,), (1,)), ((), ())), preferred_element_type=F32)
            s = jnp.where(valid, s * (HEAD_DIM ** -0.5), -jnp.inf)
            m = jnp.max(s, axis=-1, keepdims=True)
            e = jnp.exp(s - m)
            den = jnp.sum(e, axis=-1, keepdims=True)
            outs.append(jnp.dot(e.astype(BF16), vcat, preferred_element_type=F32) / den)
            lses.append(jnp.broadcast_to(m + jnp.log(den), (BAND, LANES)))
        o_ref[0, :, cs] = jnp.where(first_head, outs[0], outs[1])
        lse_ref[0, :, cs] = jnp.where(first_head, lses[0], lses[1])


def band_attn(qkv, cos, sin, dil):
    S, L, W = qkv.shape
    D = W // 3
    nb = L // BAND
    col = lambda m: pl.BlockSpec((1, BAND, D), lambda s, n: (s, n, m))
    tab = pl.BlockSpec((BAND, LANES), lambda s, n: ((s % dil) * nb + n, 0))
    out = jax.ShapeDtypeStruct((S, L, D), F32)
    return pl.pallas_call(
        _band_attn_kernel,
        grid=(S, nb),
        in_specs=[col(0), col(1), col(2), tab, tab],
        out_specs=[col(0)] * 3,
        out_shape=[out] * 3,
        scratch_shapes=[pltpu.VMEM((BAND, D), BF16), pltpu.VMEM((BAND, D), BF16)],
        compiler_params=_params("parallel", "arbitrary"),
        name="band_attn",
    )(qkv, qkv, qkv, cos, sin)


def _sample_attn_kernel(q_ref, kn_ref, vn_ref, ck0_ref, cv0_ref, ck1_ref, cv1_ref, ck2_ref, cv2_ref,
                        o_ref, *, n_new):
    scale = HEAD_DIM ** -0.5
    caches = ((ck0_ref, cv0_ref), (ck1_ref, cv1_ref), (ck2_ref, cv2_ref))
    jcol = lax.broadcasted_iota(jnp.int32, (SUBLANES, 1), 0)
    masks = []
    for ck_ref, _ in caches:
        rows = ck_ref.shape[-1]
        dil = rows // BAND
        qi = lax.broadcasted_iota(jnp.int32, (SUBLANES, rows), 0)
        ri = lax.broadcasted_iota(jnp.int32, (SUBLANES, rows), 1)
        masks.append(ri >= qi if dil == 1 else ri % dil == qi)

    def head(h, carry):
        sc, sn = [], []
        for g, (ck_ref, _) in enumerate(caches):
            q = q_ref[0, g, h]
            s = jnp.dot(q.astype(BF16), ck_ref[0, 0, h].astype(BF16), preferred_element_type=F32) * scale
            sc.append(jnp.where(masks[g], s, -jnp.inf))
            kn = kn_ref[0, g, h]
            for i in range(n_new):
                si = jnp.sum(q * kn[i:i + 1, :], axis=-1, keepdims=True) * scale
                sn.append(jnp.where(jcol >= i if g == 0 else jcol == i, si, -jnp.inf))
        m = functools.reduce(jnp.maximum, [jnp.max(s, axis=-1, keepdims=True) for s in sc] + sn)
        den = jnp.zeros((SUBLANES, 1), F32)
        acc = jnp.zeros((SUBLANES, HEAD_DIM), F32)
        for g, (_, cv_ref) in enumerate(caches):
            e = jnp.exp(sc[g] - m)
            den = den + jnp.sum(e, axis=-1, keepdims=True)
            acc = acc + lax.dot_general(e.astype(BF16), cv_ref[0, 0, h].astype(BF16),
                                        (((1,), (1,)), ((), ())), preferred_element_type=F32)
            vn = vn_ref[0, g, h]
            for i in range(n_new):
                en = jnp.exp(sn[g * n_new + i] - m)
                den = den + en
                acc = acc + en * vn[i:i + 1, :]
        o_ref[0, h] = acc / den
        return carry

    lax.fori_loop(0, N_HEADS, head, 0)


def sample_attn(q, kn, vn, caches_k, caches_v, n_new):
    B = q.shape[0]
    new_spec = pl.BlockSpec((1, N_GROUPS, N_HEADS, SUBLANES, HEAD_DIM), lambda b: (b, 0, 0, 0, 0))
    specs = [new_spec] * 3
    args = [q, kn, vn]
    for g in range(N_GROUPS):
        rows = caches_k[g].shape[2]
        assert rows == BAND * DILATIONS[g] and n_new <= SUBLANES and (g == 0 or n_new <= DILATIONS[g])
        spec = pl.BlockSpec((1, 1, N_HEADS, HEAD_DIM, rows), lambda b: (0, b, 0, 0, 0))
        specs += [spec, spec]
        args += [jnp.transpose(caches_k[g], (0, 1, 3, 4, 2)), jnp.transpose(caches_v[g], (0, 1, 3, 4, 2))]
    return pl.pallas_call(
        functools.partial(_sample_attn_kernel, n_new=n_new),
        grid=(B,),
        in_specs=specs,
        out_specs=pl.BlockSpec((1, N_HEADS, SUBLANES, HEAD_DIM), lambda b: (b, 0, 0, 0)),
        out_shape=jax.ShapeDtypeStruct((B, N_HEADS, SUBLANES, HEAD_DIM), F32),
        compiler_params=_params("parallel"),
        name="sample_attn",
    )(*args)


def _ffn_kernel(xp_ref, xs_ref, halo_ref, g_ref, wg_ref, wv_ref, cw_ref, cb_ref, wd_ref, e1_ref, e2_ref, gf_ref,
                op_ref, os_ref, hn_scr, hh_scr, acc_scr, *, seq_tiles, sample_tile, seq_new, final_norm):
    i = pl.program_id(0)
    j = pl.program_id(1)
    is_sample = i == sample_tile

    def norm(x):
        return (x * lax.rsqrt(jnp.mean(x * x, axis=-1, keepdims=True) + NORM_EPS) * g_ref[...]).astype(BF16)

    def x_tile():
        return jnp.where(is_sample, xs_ref[...], xp_ref[...])

    @pl.when(j == 0)
    def _():
        hn_scr[...] = norm(x_tile())
        hh_scr[...] = norm(halo_ref[...]).astype(F32)
        acc_scr[...] = jnp.zeros_like(acc_scr)

    hn = hn_scr[...]
    gate = jnp.dot(hn, wg_ref[...], preferred_element_type=F32)
    val = jnp.dot(hn, wv_ref[...], preferred_element_type=F32)
    gh = jnp.dot(hh_scr[...].astype(BF16), wg_ref[...], preferred_element_type=F32)
    gh = jnp.where(i % seq_tiles == 0, 0.0, gh)
    row = lax.broadcasted_iota(jnp.int32, gate.shape, 0)
    roll1 = pltpu.roll(gate, 1, axis=0)
    roll2 = pltpu.roll(gate, 2, axis=0)
    pos = row % seq_new
    prev1 = jnp.where(is_sample,
                      jnp.where(pos == 0, e1_ref[...], roll1),
                      jnp.where(row == 0, gh[7:8], roll1))
    prev2 = jnp.where(is_sample,
                      jnp.where(pos < 2, e2_ref[...], roll2),
                      jnp.where(row == 0, gh[6:7], jnp.where(row == 1, gh[7:8], roll2)))
    conv = cb_ref[...] + prev2 * cw_ref[0:1] + prev1 * cw_ref[1:2] + gate * cw_ref[2:3]
    act = (conv * jax.nn.sigmoid(conv) * val).astype(BF16)
    acc_scr[...] += jnp.dot(act, wd_ref[...], preferred_element_type=F32)

    @pl.when(j == pl.num_programs(1) - 1)
    def _():
        y = x_tile() + acc_scr[...]
        if final_norm:
            y = y * lax.rsqrt(jnp.mean(y * y, axis=-1, keepdims=True) + NORM_EPS) * gf_ref[...]

        @pl.when(is_sample)
        def _():
            os_ref[...] = y

        @pl.when(jnp.logical_not(is_sample))
        def _():
            op_ref[...] = y


def conv_glu_ffn(xp, xs, g, w_up, conv_w, conv_b, w_down, e1, e2, g_final, seq_len, seq_new, tm=512, tf=256):
    Mp, D = xp.shape
    F = w_down.shape[0]
    assert Mp % tm == 0 and F % tf == 0 and seq_len % tm == 0 and tm % seq_new == 0
    assert xs.shape == (tm, D) and e1.shape == (tm, F) and e2.shape == (tm, F)
    nf = F // tf
    sample_tile = Mp // tm
    prow = pl.BlockSpec((tm, D), lambda i, j: (jnp.minimum(i, sample_tile - 1), 0))
    srow = pl.BlockSpec((tm, D), lambda i, j: (0, 0))
    halo = pl.BlockSpec((SUBLANES, D),
                        lambda i, j: (jnp.clip(i * (tm // SUBLANES) - 1, 0, Mp // SUBLANES - 1), 0))
    vec = pl.BlockSpec((1, D), lambda i, j: (0, 0))
    fill = pl.BlockSpec((tm, tf), lambda i, j: (0, jnp.where(i == sample_tile, j, 0)))
    w_up = w_up.astype(BF16)
    return pl.pallas_call(
        functools.partial(_ffn_kernel, seq_tiles=seq_len // tm, sample_tile=sample_tile, seq_new=seq_new,
                          final_norm=g_final is not None),
        grid=(sample_tile + 1, nf),
        in_specs=[prow, srow, halo, vec,
                  pl.BlockSpec((D, tf), lambda i, j: (0, j)),
                  pl.BlockSpec((D, tf), lambda i, j: (0, j + nf)),
                  pl.BlockSpec((CONV_W, tf), lambda i, j: (0, j)),
                  pl.BlockSpec((1, tf), lambda i, j: (0, j)),
                  pl.BlockSpec((tf, D), lambda i, j: (j, 0)),
                  fill, fill, vec],
        out_specs=[prow, srow],
        out_shape=[jax.ShapeDtypeStruct((Mp, D), F32), jax.ShapeDtypeStruct((tm, D), F32)],
        scratch_shapes=[pltpu.VMEM((tm, D), BF16), pltpu.VMEM((SUBLANES, D), F32), pltpu.VMEM((tm, D), F32)],
        compiler_params=_params("arbitrary", "arbitrary"),
        name="conv_glu_ffn",
    )(xp, xs, xp, g.reshape(1, D), w_up, w_up, conv_w, conv_b.reshape(1, F), w_down.astype(BF16), e1, e2,
      (g if g_final is None else g_final).reshape(1, D))


def _rmsnorm(x, g):
    return x * lax.rsqrt(jnp.mean(x * x, axis=-1, keepdims=True) + NORM_EPS) * g


def _rope_angles(pos):
    half = HEAD_DIM // 2
    inv = jnp.exp(-math.log(ROPE_THETA) * jnp.arange(half, dtype=F32) * 2.0 / HEAD_DIM)
    return pos.astype(F32)[:, None] * inv[None, :]


def _rope_tables(pos):
    ang = _rope_angles(pos)
    cos, sin = jnp.cos(ang), jnp.sin(ang)
    return jnp.tile(cos, (1, LANES // cos.shape[1])), jnp.tile(jnp.concatenate([-sin, sin], axis=1), (1, 2))


def _rope(x, pos):
    half = HEAD_DIM // 2
    ang = _rope_angles(pos)
    cos = jnp.cos(ang)[:, None, None, :]
    sin = jnp.sin(ang)[:, None, None, :]
    x1, x2 = x[..., :half], x[..., half:]
    return jnp.concatenate([x1 * cos - x2 * sin, x2 * cos + x1 * sin], axis=-1)


def kernel(x_prompt, x_sample, state_rwkv_shift, state_rwkv_wkv, cache_k_w128, cache_v_w128, cache_k_w512, cache_v_w512, cache_k_w2048, cache_v_w2048, state_ffn_conv, norm_mix, norm_ffn, norm_final, rwkv_mu, rwkv_w_rkv, rwkv_w0, rwkv_w1, rwkv_w2, rwkv_a0, rwkv_a1, rwkv_a2, rwkv_g1, rwkv_g2, rwkv_k_k, rwkv_k_a, rwkv_r_k, rwkv_ln_w, rwkv_ln_b, rwkv_w_o, attn_w_in, attn_w_o, ffn_w_up, ffn_conv_w, ffn_conv_b, ffn_w_down):
    Bp, Tp, D = x_prompt.shape
    Bs, Ts, _ = x_sample.shape
    Mp, Ms = Bp * Tp, Bs * Ts
    assert Ts >= CONV_W - 1 and Tp >= CONV_W - 1 and CONV_W == 3
    xp = x_prompt.reshape(Mp, D)
    xs = x_sample.reshape(Ms, D)
    caches_k = (cache_k_w128, cache_k_w512, cache_k_w2048)
    caches_v = (cache_v_w128, cache_v_w512, cache_v_w2048)

    p_shift, p_wkv, s_shift, s_wkv = [], [], [], []
    p_k = [[] for _ in range(N_GROUPS)]
    p_v = [[] for _ in range(N_GROUPS)]
    s_k = [[] for _ in range(N_GROUPS)]
    s_v = [[] for _ in range(N_GROUPS)]
    p_conv, s_conv = [], []

    def time_mix(li, h, x_last, s0, B, T, steps):
        h3 = h.reshape(B, T, D)
        prev = jnp.concatenate([x_last[:, None, :], h3[:, :-1]], axis=1).reshape(B * T, D)
        xx = prev - h
        mix = [(h + xx * rwkv_mu[li, n]).astype(BF16) for n in range(6)]
        r = mm(mix[0], rwkv_w_rkv[li, 0])
        k = mm(mix[1], rwkv_w_rkv[li, 1])
        v = mm(mix[2], rwkv_w_rkv[li, 2])
        w_log = -jax.nn.softplus(-(rwkv_w0[li] + mm(jnp.tanh(mm(mix[3], rwkv_w1[li])), rwkv_w2[li]))) - 0.5
        decay = jnp.exp(-jnp.exp(w_log))
        a = jax.nn.sigmoid(rwkv_a0[li] + mm(mm(mix[4], rwkv_a1[li]), rwkv_a2[li]))
        g = mm(jax.nn.sigmoid(mm(mix[5], rwkv_g1[li])), rwkv_g2[li])
        heads = lambda t: t.reshape(-1, N_HEADS, HEAD_DIM)
        kk = heads(k * rwkv_k_k[li])
        kk = (kk / jnp.maximum(jnp.sqrt(jnp.sum(kk * kk, axis=-1, keepdims=True)), 1e-12)).reshape(-1, D)
        k = k * (1.0 + (a - 1.0) * rwkv_k_a[li])
        y, st = _wkv(r, decay, k, v, -kk, kk * a, s0, B, T, steps)
        y = heads(y)
        mean = jnp.mean(y, axis=-1, keepdims=True)
        var = jnp.mean(jnp.square(y - mean), axis=-1, keepdims=True)
        y = ((y - mean) * lax.rsqrt(var + GN_EPS)).reshape(-1, D) * rwkv_ln_w[li] + rwkv_ln_b[li]
        bonus = jnp.sum(heads(r) * heads(k) * rwkv_r_k[li], axis=-1, keepdims=True) * heads(v)
        y = y + bonus.reshape(-1, D)
        return mm((y * g).astype(BF16), rwkv_w_o[li]), st

    depth = norm_mix.shape[0]
    for i in range(depth):
        li = i // 2
        hp = _rmsnorm(xp, norm_mix[i])
        hs = _rmsnorm(xs, norm_mix[i])
        if i % 2 == 0:
            mp, stp = time_mix(li, hp, jnp.zeros((Bp, D), F32),
                               jnp.zeros((Bp, N_HEADS, HEAD_DIM, HEAD_DIM), F32), Bp, Tp, 32)
            ms, sts = time_mix(li, hs, state_rwkv_shift[li], state_rwkv_wkv[li], Bs, Ts, Ts)
            p_shift.append(hp.reshape(Bp, Tp, D)[:, -1])
            s_shift.append(hs.reshape(Bs, Ts, D)[:, -1])
            p_wkv.append(stp)
            s_wkv.append(sts)
        else:
            assert len(caches_k[0].shape) == 5 and caches_k[0].shape[0] == 1 and li == 0
            hb = hp.astype(BF16)
            outs, lses = [], []
            for g in range(N_GROUPS):
                dil = DILATIONS[g]
                L = Tp // dil
                keep = min(WINDOWS[g], Tp)
                assert Tp % (dil * BAND) == 0 and keep % dil == 0
                classes = lambda t: t.reshape(Bp, L, dil, -1).transpose(0, 2, 1, 3)
                tokens = lambda t: t.reshape(Bp, dil, L, -1).transpose(0, 2, 1, 3)
                hg = classes(hb).reshape(Mp, D)
                qkv_g = mm(hg, attn_w_in[li][:, g * 3 * D:(g + 1) * 3 * D]).reshape(Bp * dil, L, 3 * D)
                pos = (jnp.arange(dil)[:, None] + dil * jnp.arange(L)[None, :]).reshape(-1)
                o, lse, kr = band_attn(qkv_g, *_rope_tables(pos), dil)
                outs.append(tokens(o).reshape(Bp, Tp, D))
                lses.append(tokens(lse).reshape(Bp, Tp, D))
                tail = lambda t: t.reshape(Bp, dil, L, D)[:, :, L - keep // dil:].transpose(0, 2, 1, 3).reshape(
                    Bp, keep, N_HEADS, HEAD_DIM)
                p_k[g].append(tail(kr))
                p_v[g].append(tail(qkv_g[..., 2 * D:]))
            wts = jax.nn.softmax(jnp.stack(lses, axis=0), axis=0)
            op = sum(wts[g] * outs[g] for g in range(N_GROUPS))
            qkv_s = mm(hs.astype(BF16), attn_w_in[li]).reshape(Bs, Ts, N_GROUPS, 3, N_HEADS, HEAD_DIM)
            pos_s = PAST_LEN + jnp.arange(Ts)
            q_s = _rope(qkv_s[:, :, :, 0], pos_s)
            k_s = _rope(qkv_s[:, :, :, 1], pos_s)
            v_s = qkv_s[:, :, :, 2]
            rows8 = lambda t: jnp.pad(t.transpose(0, 2, 3, 1, 4), ((0, 0),) * 3 + ((0, SUBLANES - Ts), (0, 0)))
            osmp = sample_attn(rows8(q_s), rows8(k_s), rows8(v_s), caches_k, caches_v, Ts)
            osmp = osmp[:, :, :Ts].transpose(0, 2, 1, 3).reshape(Ms, D)
            for g in range(N_GROUPS):
                s_k[g].append(k_s[:, :, g])
                s_v[g].append(v_s[:, :, g])
            mp = mm(op.reshape(Mp, D).astype(BF16), attn_w_o[li])
            ms = mm(osmp, attn_w_o[li])
        xp = xp + mp
        xs = xs + ms

        buf = state_ffn_conv[i]
        first = (jnp.arange(Ms) % Ts == 0)[:, None]
        e1 = jnp.repeat(buf[:, 1], Ts, axis=0)
        e2 = jnp.where(first, jnp.repeat(buf[:, 0], Ts, axis=0), e1)
        tail = jnp.concatenate([xp.reshape(Bp, Tp, D)[:, Tp - 2:].reshape(Bp * 2, D),
                                xs.reshape(Bs, Ts, D)[:, Ts - 2:].reshape(Bs * 2, D)], axis=0)
        gate_tail = mm(_rmsnorm(tail, norm_ffn[i]), ffn_w_up[i][:, :D_FF])
        p_conv.append(gate_tail[:Bp * 2].reshape(Bp, 2, D_FF))
        s_conv.append(gate_tail[Bp * 2:].reshape(Bs, 2, D_FF))
        xp, xs = conv_glu_ffn(xp, xs, norm_ffn[i], ffn_w_up[i], ffn_conv_w[i], ffn_conv_b[i], ffn_w_down[i],
                              e1, e2, norm_final if i == depth - 1 else None, Tp, Ts)

    y_prompt = xp.reshape(Bp, Tp, D)
    y_sample = xs.reshape(Bs, Ts, D)
    st = jnp.stack
    return (y_prompt, y_sample,
            st(p_shift), st(p_wkv),
            st(p_k[0]), st(p_v[0]), st(p_k[1]), st(p_v[1]), st(p_k[2]), st(p_v[2]), st(p_conv),
            st(s_shift), st(s_wkv),
            st(s_k[0]), st(s_v[0]), st(s_k[1]), st(s_v[1]), st(s_k[2]), st(s_v[2]), st(s_conv))
```

```python
import functools
import math

import jax
import jax.numpy as jnp
from jax import lax
from jax.experimental import pallas as pl
from jax.experimental.pallas import tpu as pltpu

F32 = jnp.float32
BF16 = jnp.bfloat16

D_MODEL = 1024
HEAD_DIM = 64
N_HEADS = 16
D_FF = 2816
CONV_W = 3
N_GROUPS = 3
WINDOWS = (128, 512, 2048)
DILATIONS = (1, 4, 16)
BAND = 128
GN_EPS = 64e-5
NORM_EPS = 1e-6
ROPE_THETA = 10000.0
PAST_LEN = 2048
LANES = 128
SUBLANES = 8
VMEM_LIMIT = 56 * 1024 * 1024


def _params(*sem):
    return pltpu.CompilerParams(dimension_semantics=sem, vmem_limit_bytes=VMEM_LIMIT)


def _mm_kernel(x_ref, w_ref, o_ref):
    o_ref[...] = jnp.dot(x_ref[...].astype(BF16), w_ref[...],
                         preferred_element_type=F32).astype(o_ref.dtype)


def mm(x, w, tm=1024, tn=1024):
    M, K = x.shape
    N = w.shape[1]
    tm = next(t for t in (tm, tm // 2, M) if M % t == 0)
    tn = next(t for t in (tn, tn // 2, N) if N % t == 0)
    return pl.pallas_call(
        _mm_kernel,
        grid=(M // tm, N // tn),
        in_specs=[pl.BlockSpec((tm, K), lambda i, j: (i, 0)),
                  pl.BlockSpec((K, tn), lambda i, j: (0, j))],
        out_specs=pl.BlockSpec((tm, tn), lambda i, j: (i, j)),
        out_shape=jax.ShapeDtypeStruct((M, N), F32),
        compiler_params=_params("parallel", "parallel"),
        name="mm",
    )(x, w.astype(BF16))


def _wkv_kernel(a_ref, wr_ref, w_ref, b_ref, k_ref, v_ref, br_ref, kr_ref, s0_ref,
                y_ref, st_ref, state, *, steps):
    @pl.when(pl.program_id(1) == 0)
    def _():
        state[...] = s0_ref[...]

    def row(ref, t, k):
        return jnp.broadcast_to(ref[t, pl.ds(k, 1), :], (SUBLANES, LANES))[None]

    def step(t, carry):
        sa = jnp.zeros((4, SUBLANES, LANES), F32)
        yy = jnp.zeros((4, SUBLANES, LANES), F32)
        for k in range(HEAD_DIM):
            sk = state[k]
            sa = sa + sk * row(a_ref, t, k)
            yy = yy + sk * row(wr_ref, t, k)
        v = v_ref[t]
        y_ref[t] = yy + sa * row(br_ref, t, 0) + v * row(kr_ref, t, 0)
        for k in range(HEAD_DIM):
            state[k] = state[k] * row(w_ref, t, k) + sa * row(b_ref, t, k) + v * row(k_ref, t, k)
        return carry

    lax.fori_loop(0, steps, step, 0)

    @pl.when(pl.program_id(1) == pl.num_programs(1) - 1)
    def _():
        st_ref[...] = state[...]


def wkv_scan(rows, v, br, kr, s0, steps):
    T, _, L = rows[0].shape
    nl = L // LANES
    assert T % steps == 0
    row_spec = pl.BlockSpec((steps, HEAD_DIM, LANES), lambda l, t: (t, 0, l))
    col_spec = pl.BlockSpec((steps, 4, SUBLANES, LANES), lambda l, t: (t, 0, 0, l))
    sc_spec = pl.BlockSpec((steps, 1, LANES), lambda l, t: (t, 0, l))
    st_spec = pl.BlockSpec((HEAD_DIM, 4, SUBLANES, LANES), lambda l, t: (0, 0, 0, l))
    return pl.pallas_call(
        functools.partial(_wkv_kernel, steps=steps),
        grid=(nl, T // steps),
        in_specs=[row_spec] * 5 + [col_spec, sc_spec, sc_spec, st_spec],
        out_specs=[col_spec, st_spec],
        out_shape=[jax.ShapeDtypeStruct(v.shape, F32), jax.ShapeDtypeStruct(s0.shape, F32)],
        scratch_shapes=[pltpu.VMEM((HEAD_DIM, 4, SUBLANES, LANES), F32)],
        compiler_params=_params("parallel", "arbitrary"),
        name="wkv_scan",
    )(*rows, v, br, kr, s0)


CHUNK = LANES
KEYS_PER_TRIP = 16


def _wkv_chunk_kernel(a_ref, wr_ref, w_ref, b_ref, k_ref, v_ref, sc_ref, y_ref, st_ref,
                      state, zrow, zv, zs, zy, xt):
    c = pl.program_id(0)
    nb = a_ref.shape[0]
    rows_bh = nb * N_HEADS

    @pl.when(c == 0)
    def _():
        state[...] = jnp.zeros_like(state)

    def to_channel_major(ref):
        def body(b, carry):
            for cb in range(ref.shape[2] // LANES):
                xt[b, cb * LANES:(cb + 1) * LANES, :] = ref[b, :, cb * LANES:(cb + 1) * LANES].T
            return carry
        lax.fori_loop(0, nb, body, 0)

    def lanes_tile(first, second, stride):
        top = xt[:, pl.ds(first, N_HEADS, stride=stride), :].reshape(rows_bh, CHUNK)
        bot = xt[:, pl.ds(second, N_HEADS, stride=stride), :].reshape(rows_bh, CHUNK)
        return jnp.concatenate([top, bot], axis=0).T

    def grouped(n, body):
        def trip(g, carry):
            for u in range(SUBLANES):
                body(g * SUBLANES + u)
            return carry
        lax.fori_loop(0, n // SUBLANES, trip, 0)

    for op, ref in enumerate((a_ref, wr_ref, w_ref, b_ref, k_ref)):
        to_channel_major(ref)

        def fill(k, op=op):
            zrow[op, k] = lanes_tile(k, k, HEAD_DIM)
        grouped(HEAD_DIM, fill)

    to_channel_major(v_ref)

    def fill_v(v_lo):
        zv[pl.ds(pl.multiple_of(v_lo * CHUNK, CHUNK), CHUNK), :] = lanes_tile(v_lo, HEAD_DIM // 2 + v_lo, HEAD_DIM)
    grouped(HEAD_DIM // 2, fill_v)

    to_channel_major(sc_ref)
    zs[0] = lanes_tile(0, 0, 1)
    zs[1] = lanes_tile(N_HEADS, N_HEADS, 1)

    def row(op, k, t):
        return jnp.broadcast_to(zrow[op, k, pl.ds(t, 1), :], (SUBLANES, LANES))[None]

    def step(t, carry):
        def reduce_keys(g, acc):
            sa, yy = acc
            for u in range(KEYS_PER_TRIP):
                k = g * KEYS_PER_TRIP + u
                sk = state[k]
                sa = sa + sk * row(0, k, t)
                yy = yy + sk * row(1, k, t)
            return sa, yy
        zero = jnp.zeros((4, SUBLANES, LANES), F32)
        sa, yy = lax.fori_loop(0, HEAD_DIM // KEYS_PER_TRIP, reduce_keys, (zero, zero))
        v = zv[pl.ds(t, HEAD_DIM // 2, stride=CHUNK), :].reshape(4, SUBLANES, LANES)
        br = jnp.broadcast_to(zs[0, pl.ds(t, 1), :], (SUBLANES, LANES))[None]
        kr = jnp.broadcast_to(zs[1, pl.ds(t, 1), :], (SUBLANES, LANES))[None]
        zy[pl.ds(t, HEAD_DIM // 2, stride=CHUNK), :] = (yy + sa * br + v * kr).reshape(HEAD_DIM // 2, LANES)

        def update_keys(g, carry):
            for u in range(KEYS_PER_TRIP):
                k = g * KEYS_PER_TRIP + u
                state[k] = state[k] * row(2, k, t) + sa * row(3, k, t) + v * row(4, k, t)
            return carry
        lax.fori_loop(0, HEAD_DIM // KEYS_PER_TRIP, update_keys, 0)
        return carry

    lax.fori_loop(0, CHUNK, step, 0)

    def drain(v_lo):
        tile = zy[pl.ds(pl.multiple_of(v_lo * CHUNK, CHUNK), CHUNK), :].T
        for v_hi in range(2):
            xt[:, pl.ds(v_hi * (HEAD_DIM // 2) + v_lo, N_HEADS, stride=HEAD_DIM), :] = (
                tile[v_hi * rows_bh:(v_hi + 1) * rows_bh].reshape(nb, N_HEADS, CHUNK))
    grouped(HEAD_DIM // 2, drain)

    def to_token_major(b, carry):
        for cb in range(y_ref.shape[2] // LANES):
            y_ref[b, :, cb * LANES:(cb + 1) * LANES] = xt[b, cb * LANES:(cb + 1) * LANES, :].T
        return carry
    lax.fori_loop(0, nb, to_token_major, 0)

    @pl.when(c == pl.num_programs(0) - 1)
    def _():
        st_ref[...] = state[...]


def wkv_scan_tokens(ops, sc):
    B, T, D = ops[0].shape
    assert B * N_HEADS * 2 == LANES and T % CHUNK == 0 and D == N_HEADS * HEAD_DIM
    once = pl.Buffered(1)
    spec = pl.BlockSpec((B, CHUNK, D), lambda c: (0, c, 0), pipeline_mode=once)
    st_shape = (HEAD_DIM, 4, SUBLANES, LANES)
    return pl.pallas_call(
        _wkv_chunk_kernel,
        grid=(T // CHUNK,),
        in_specs=[spec] * 6 + [pl.BlockSpec((B, CHUNK, LANES), lambda c: (0, c, 0), pipeline_mode=once)],
        out_specs=[spec, pl.BlockSpec(st_shape, lambda c: (0, 0, 0, 0))],
        out_shape=[jax.ShapeDtypeStruct((B, T, D), F32), jax.ShapeDtypeStruct(st_shape, F32)],
        scratch_shapes=[pltpu.VMEM(st_shape, F32),
                        pltpu.VMEM((5, HEAD_DIM, CHUNK, LANES), F32),
                        pltpu.VMEM((HEAD_DIM // 2 * CHUNK, LANES), F32),
                        pltpu.VMEM((2, CHUNK, LANES), F32),
                        pltpu.VMEM((HEAD_DIM // 2 * CHUNK, LANES), F32),
                        pltpu.VMEM((B, D, CHUNK), F32)],
        compiler_params=_params("arbitrary"),
        name="wkv_scan_tokens",
    )(*ops, sc)


def _to_rows(x, B, T):
    nl = B * N_HEADS // 64
    x = x.reshape(B, T, N_HEADS, HEAD_DIM).transpose(1, 3, 0, 2).reshape(T, HEAD_DIM, nl, 1, 64)
    return jnp.broadcast_to(x, (T, HEAD_DIM, nl, 2, 64)).reshape(T, HEAD_DIM, nl * LANES)


def _to_cols(x, B, T):
    nl = B * N_HEADS // 64
    x = x.reshape(B, T, N_HEADS, 2, 32).transpose(1, 4, 0, 2, 3).reshape(T, 32, nl, 64, 2)
    return x.transpose(0, 1, 2, 4, 3).reshape(T, 4, SUBLANES, nl * LANES)


def _from_cols(y, B, T):
    nl = B * N_HEADS // 64
    y = y.reshape(T, 32, nl, 2, 64).transpose(0, 2, 4, 3, 1)
    return y.reshape(T, B, N_HEADS, HEAD_DIM).transpose(1, 0, 2, 3).reshape(B * T, D_MODEL)


def _head_scalar(x, B, T):
    nl = B * N_HEADS // 64
    x = x.reshape(B, T, N_HEADS).transpose(1, 0, 2).reshape(T, nl, 1, 64)
    return jnp.broadcast_to(x, (T, nl, 2, 64)).reshape(T, 1, nl * LANES)


def _state_in(s, B):
    nl = B * N_HEADS // 64
    s = s.reshape(B, N_HEADS, 2, 32, HEAD_DIM).transpose(4, 3, 0, 1, 2).reshape(HEAD_DIM, 32, nl, 64, 2)
    return s.transpose(0, 1, 2, 4, 3).reshape(HEAD_DIM, 4, SUBLANES, nl * LANES)


def _state_out(s, B):
    nl = B * N_HEADS // 64
    s = s.reshape(HEAD_DIM, 32, nl, 2, 64).transpose(2, 4, 3, 1, 0)
    return s.reshape(B, N_HEADS, HEAD_DIM, HEAD_DIM)


def _wkv(a, wr, w, b, k, v, br, kr, s0, B, T, steps):
    rows = [_to_rows(x, B, T) for x in (a, wr, w, b, k)]
    y, st = wkv_scan(rows, _to_cols(v, B, T), _head_scalar(br, B, T), _head_scalar(kr, B, T),
                     _state_in(s0, B), steps)
    return _from_cols(y, B, T), _state_out(st, B)


def _dot_split(x, m):
    hi = x.astype(BF16)
    lo = (x - hi.astype(F32)).astype(BF16)
    return jnp.dot(hi, m, preferred_element_type=F32) + jnp.dot(lo, m, preferred_element_type=F32)


def _head_matrices():
    lane_head = jnp.arange(D_MODEL)[:, None] // HEAD_DIM
    col = jnp.arange(LANES)[None, :]
    seg = jnp.stack([(col == lane_head + N_HEADS * j) for j in range(3)]).astype(BF16)
    return seg, jnp.swapaxes(seg, 1, 2)


def _rwkv_prep_kernel(x_ref, halo_ref, e_ref, g_ref, mu_ref, wrkv_ref, w1_ref, w2_ref, a1_ref, a2_ref, g1_ref,
                      g2_ref, vec_ref, seg_ref, spread_ref,
                      a_out, wr_out, w_out, b_out, k_out, v_out, gate_out, sc_out, *, seq_tiles, seq_new):
    i = pl.program_id(0)

    def norm(x):
        return x * lax.rsqrt(jnp.mean(x * x, axis=-1, keepdims=True) + NORM_EPS) * g_ref[...]

    h = norm(x_ref[...])
    row = lax.broadcasted_iota(jnp.int32, h.shape, 0)
    shifted = pltpu.roll(h, 1, axis=0)
    if seq_new:
        prev = jnp.where(row % seq_new == 0, e_ref[...], shifted)
    else:
        before = jnp.where(i % seq_tiles == 0, 0.0, norm(halo_ref[...])[SUBLANES - 1:])
        prev = jnp.where(row == 0, before, shifted)
    xx = prev - h
    mix = lambda n: (h + xx * mu_ref[n:n + 1]).astype(BF16)
    proj = lambda x, w: jnp.dot(x, w, preferred_element_type=F32)
    w0, a0, k_k, k_a, r_k = (vec_ref[n:n + 1] for n in range(5))
    r = proj(mix(0), wrkv_ref[0])
    k = proj(mix(1), wrkv_ref[1])
    v = proj(mix(2), wrkv_ref[2])
    z = -(w0 + proj(jnp.tanh(proj(mix(3), w1_ref[...])).astype(BF16), w2_ref[...]))
    softplus = jnp.maximum(z, 0.0) + jnp.log(1.0 + jnp.exp(-jnp.abs(z)))
    w = jnp.exp(-jnp.exp(-softplus - 0.5))
    a = jax.nn.sigmoid(a0 + proj(proj(mix(4), a1_ref[...]).astype(BF16), a2_ref[...]))
    gate_out[...] = proj(jax.nn.sigmoid(proj(mix(5), g1_ref[...])).astype(BF16), g2_ref[...])
    kk = k * k_k
    norm2 = _dot_split(_dot_split(kk * kk, seg_ref[0]), spread_ref[0])
    kk = kk / jnp.maximum(jnp.sqrt(norm2), 1e-12)
    k = k * (1.0 + (a - 1.0) * k_a)
    b = kk * a
    a_out[...] = -kk
    wr_out[...] = w * r
    w_out[...] = w
    b_out[...] = b
    k_out[...] = k
    v_out[...] = v
    sc_out[...] = (_dot_split(b * r, seg_ref[0]) + _dot_split(k * r, seg_ref[1])
                   + _dot_split(r * k * r_k, seg_ref[2]))


def rwkv_prep(x, e, g, mu, w_rkv, w1, w2, a1, a2, g1, g2, vec, seq_len, seq_new, tm=256):
    M, D = x.shape
    assert M % tm == 0 and (tm % seq_new == 0 if seq_new else seq_len % tm == 0)
    seg, spread = _head_matrices()
    once = pl.Buffered(1)
    row = pl.BlockSpec((tm, D), lambda i: (i, 0))
    halo = pl.BlockSpec((SUBLANES, D), lambda i: (jnp.maximum(i * (tm // SUBLANES) - 1, 0), 0))
    full = lambda t: pl.BlockSpec(t.shape, lambda i: (0,) * t.ndim, pipeline_mode=once)
    consts = [g.reshape(1, D), mu] + [t.astype(BF16) for t in (w_rkv, w1, w2, a1, a2, g1, g2)] + [vec, seg, spread]
    out = jax.ShapeDtypeStruct((M, D), F32)
    return pl.pallas_call(
        functools.partial(_rwkv_prep_kernel, seq_tiles=max(seq_len // tm, 1), seq_new=seq_new),
        grid=(M // tm,),
        in_specs=[row, halo, row] + [full(t) for t in consts],
        out_specs=[row] * 7 + [pl.BlockSpec((tm, LANES), lambda i: (i, 0))],
        out_shape=[out] * 7 + [jax.ShapeDtypeStruct((M, LANES), F32)],
        compiler_params=_params("parallel"),
        name="rwkv_prep",
    )(x, x, e, *consts)


def _rwkv_post_kernel(y_ref, v_ref, gate_ref, sc_ref, x_ref, wo_ref, lnw_ref, lnb_ref, seg_ref, spread_ref, o_ref):
    y = y_ref[...]
    inv = 1.0 / HEAD_DIM
    mean = _dot_split(_dot_split(y, seg_ref[0]), spread_ref[0]) * inv
    d = y - mean
    var = _dot_split(_dot_split(d * d, seg_ref[0]), spread_ref[0]) * inv
    yn = d * lax.rsqrt(var + GN_EPS) * lnw_ref[...] + lnb_ref[...]
    bonus = _dot_split(sc_ref[...], spread_ref[2]) * v_ref[...]
    out = ((yn + bonus) * gate_ref[...]).astype(BF16)
    o_ref[...] = x_ref[...] + jnp.dot(out, wo_ref[...], preferred_element_type=F32)


def rwkv_post(y, v, gate, sc, x, w_o, ln_w, ln_b, tm=512):
    M, D = x.shape
    assert M % tm == 0
    seg, spread = _head_matrices()
    once = pl.Buffered(1)
    row = pl.BlockSpec((tm, D), lambda i: (i, 0))
    full = lambda t: pl.BlockSpec(t.shape, lambda i: (0,) * t.ndim, pipeline_mode=once)
    consts = [w_o.astype(BF16), ln_w.reshape(1, D), ln_b.reshape(1, D), seg, spread]
    return pl.pallas_call(
        _rwkv_post_kernel,
        grid=(M // tm,),
        in_specs=[row, row, row, pl.BlockSpec((tm, LANES), lambda i: (i, 0)), row] + [full(t) for t in consts],
        out_specs=row,
        out_shape=jax.ShapeDtypeStruct((M, D), F32),
        compiler_params=_params("parallel"),
        name="rwkv_post",
    )(y, v, gate, sc, x, *consts)


def _band_attn_kernel(q_ref, k_ref, v_ref, cos_ref, sin_ref, o_ref, lse_ref, kr_ref, kprev, vprev):
    n = pl.program_id(1)

    @pl.when(n == 0)
    def _():
        kprev[...] = jnp.zeros_like(kprev)
        vprev[...] = jnp.zeros_like(vprev)

    i = lax.broadcasted_iota(jnp.int32, (BAND, 2 * BAND), 0)
    j = lax.broadcasted_iota(jnp.int32, (BAND, 2 * BAND), 1)
    valid = (j >= i) & (j <= i + BAND) & ((n > 0) | (j >= BAND))
    lane = lax.broadcasted_iota(jnp.int32, (BAND, LANES), 1)
    first_head = lane < HEAD_DIM
    first_half = lane % HEAD_DIM < HEAD_DIM // 2
    cos, sin = cos_ref[...], sin_ref[...]

    def rope(x):
        partner = jnp.where(first_half, pltpu.roll(x, LANES - HEAD_DIM // 2, axis=1),
                            pltpu.roll(x, HEAD_DIM // 2, axis=1))
        return x * cos + partner * sin

    for p in range(N_HEADS // 2):
        cs = slice(p * LANES, (p + 1) * LANES)
        q = rope(q_ref[0, :, cs])
        k = rope(k_ref[0, :, cs])
        kr_ref[0, :, cs] = k
        kb = k.astype(BF16)
        vb = v_ref[0, :, cs].astype(BF16)
        kcat = jnp.concatenate([kprev[:, cs], kb], axis=0)
        vcat = jnp.concatenate([vprev[:, cs], vb], axis=0)
        kprev[:, cs] = kb
        vprev[:, cs] = vb
        outs, lses = [], []
        for head_mask in (first_head, jnp.logical_not(first_head)):
            qm = jnp.where(head_mask, q, 0.0).astype(BF16)
            s = lax.dot_general(qm, kcat, (((1,), (1,)), ((), ())), preferred_element_type=F32)
            s = jnp.where(valid, s * (HEAD_DIM ** -0.5), -jnp.inf)
            m = jnp.max(s, axis=-1, keepdims=True)
            e = jnp.exp(s - m)
            den = jnp.sum(e, axis=-1, keepdims=True)
            outs.append(jnp.dot(e.astype(BF16), vcat, preferred_element_type=F32) / den)
            lses.append(jnp.broadcast_to(m + jnp.log(den), (BAND, LANES)))
        o_ref[0, :, cs] = jnp.where(first_head, outs[0], outs[1])
        lse_ref[0, :, cs] = jnp.where(first_head, lses[0], lses[1])


def band_attn(qkv, cos, sin, dil):
    S, L, W = qkv.shape
    D = W // 3
    nb = L // BAND
    col = lambda m: pl.BlockSpec((1, BAND, D), lambda s, n: (s, n, m))
    tab = pl.BlockSpec((BAND, LANES), lambda s, n: ((s % dil) * nb + n, 0))
    out = jax.ShapeDtypeStruct((S, L, D), F32)
    return pl.pallas_call(
        _band_attn_kernel,
        grid=(S, nb),
        in_specs=[col(0), col(1), col(2), tab, tab],
        out_specs=[col(0)] * 3,
        out_shape=[out] * 3,
        scratch_shapes=[pltpu.VMEM((BAND, D), BF16), pltpu.VMEM((BAND, D), BF16)],
        compiler_params=_params("parallel", "arbitrary"),
        name="band_attn",
    )(qkv, qkv, qkv, cos, sin)


def _sample_attn_kernel(q_ref, kn_ref, vn_ref, ck0_ref, cv0_ref, ck1_ref, cv1_ref, ck2_ref, cv2_ref,
                        o_ref, *, n_new):
    scale = HEAD_DIM ** -0.5
    caches = ((ck0_ref, cv0_ref), (ck1_ref, cv1_ref), (ck2_ref, cv2_ref))
    jcol = lax.broadcasted_iota(jnp.int32, (SUBLANES, 1), 0)
    masks = []
    for ck_ref, _ in caches:
        rows = ck_ref.shape[-1]
        dil = rows // BAND
        qi = lax.broadcasted_iota(jnp.int32, (SUBLANES, rows), 0)
        ri = lax.broadcasted_iota(jnp.int32, (SUBLANES, rows), 1)
        masks.append(ri >= qi if dil == 1 else ri % dil == qi)

    def head(h, carry):
        sc, sn = [], []
        for g, (ck_ref, _) in enumerate(caches):
            q = q_ref[0, g, h]
            s = jnp.dot(q.astype(BF16), ck_ref[0, 0, h].astype(BF16), preferred_element_type=F32) * scale
            sc.append(jnp.where(masks[g], s, -jnp.inf))
            kn = kn_ref[0, g, h]
            for i in range(n_new):
                si = jnp.sum(q * kn[i:i + 1, :], axis=-1, keepdims=True) * scale
                sn.append(jnp.where(jcol >= i if g == 0 else jcol == i, si, -jnp.inf))
        m = functools.reduce(jnp.maximum, [jnp.max(s, axis=-1, keepdims=True) for s in sc] + sn)
        den = jnp.zeros((SUBLANES, 1), F32)
        acc = jnp.zeros((SUBLANES, HEAD_DIM), F32)
        for g, (_, cv_ref) in enumerate(caches):
            e = jnp.exp(sc[g] - m)
            den = den + jnp.sum(e, axis=-1, keepdims=True)
            acc = acc + lax.dot_general(e.astype(BF16), cv_ref[0, 0, h].astype(BF16),
                                        (((1,), (1,)), ((), ())), preferred_element_type=F32)
            vn = vn_ref[0, g, h]
            for i in range(n_new):
                en = jnp.exp(sn[g * n_new + i] - m)
                den = den + en
                acc = acc + en * vn[i:i + 1, :]
        o_ref[0, h] = acc / den
        return carry

    lax.fori_loop(0, N_HEADS, head, 0)


def sample_attn(q, kn, vn, caches_k, caches_v, n_new):
    B = q.shape[0]
    new_spec = pl.BlockSpec((1, N_GROUPS, N_HEADS, SUBLANES, HEAD_DIM), lambda b: (b, 0, 0, 0, 0))
    specs = [new_spec] * 3
    args = [q, kn, vn]
    for g in range(N_GROUPS):
        rows = caches_k[g].shape[2]
        assert rows == BAND * DILATIONS[g] and n_new <= SUBLANES and (g == 0 or n_new <= DILATIONS[g])
        spec = pl.BlockSpec((1, 1, N_HEADS, HEAD_DIM, rows), lambda b: (0, b, 0, 0, 0))
        specs += [spec, spec]
        args += [jnp.transpose(caches_k[g], (0, 1, 3, 4, 2)), jnp.transpose(caches_v[g], (0, 1, 3, 4, 2))]
    return pl.pallas_call(
        functools.partial(_sample_attn_kernel, n_new=n_new),
        grid=(B,),
        in_specs=specs,
        out_specs=pl.BlockSpec((1, N_HEADS, SUBLANES, HEAD_DIM), lambda b: (b, 0, 0, 0)),
        out_shape=jax.ShapeDtypeStruct((B, N_HEADS, SUBLANES, HEAD_DIM), F32),
        compiler_params=_params("parallel"),
        name="sample_attn",
    )(*args)


def _ffn_kernel(xp_ref, xs_ref, halo_ref, g_ref, wg_ref, wv_ref, cw_ref, cb_ref, wd_ref, e1_ref, e2_ref, gf_ref,
                op_ref, os_ref, hn_scr, hh_scr, acc_scr, *, seq_tiles, sample_tile, seq_new, final_norm):
    i = pl.program_id(0)
    j = pl.program_id(1)
    is_sample = i == sample_tile

    def norm(x):
        return (x * lax.rsqrt(jnp.mean(x * x, axis=-1, keepdims=True) + NORM_EPS) * g_ref[...]).astype(BF16)

    def x_tile():
        return jnp.where(is_sample, xs_ref[...], xp_ref[...])

    @pl.when(j == 0)
    def _():
        hn_scr[...] = norm(x_tile())
        hh_scr[...] = norm(halo_ref[...]).astype(F32)
        acc_scr[...] = jnp.zeros_like(acc_scr)

    hn = hn_scr[...]
    gate = jnp.dot(hn, wg_ref[...], preferred_element_type=F32)
    val = jnp.dot(hn, wv_ref[...], preferred_element_type=F32)
    gh = jnp.dot(hh_scr[...].astype(BF16), wg_ref[...], preferred_element_type=F32)
    gh = jnp.where(i % seq_tiles == 0, 0.0, gh)
    row = lax.broadcasted_iota(jnp.int32, gate.shape, 0)
    roll1 = pltpu.roll(gate, 1, axis=0)
    roll2 = pltpu.roll(gate, 2, axis=0)
    pos = row % seq_new
    prev1 = jnp.where(is_sample,
                      jnp.where(pos == 0, e1_ref[...], roll1),
                      jnp.where(row == 0, gh[7:8], roll1))
    prev2 = jnp.where(is_sample,
                      jnp.where(pos < 2, e2_ref[...], roll2),
                      jnp.where(row == 0, gh[6:7], jnp.where(row == 1, gh[7:8], roll2)))
    conv = cb_ref[...] + prev2 * cw_ref[0:1] + prev1 * cw_ref[1:2] + gate * cw_ref[2:3]
    act = (conv * jax.nn.sigmoid(conv) * val).astype(BF16)
    acc_scr[...] += jnp.dot(act, wd_ref[...], preferred_element_type=F32)

    @pl.when(j == pl.num_programs(1) - 1)
    def _():
        y = x_tile() + acc_scr[...]
        if final_norm:
            y = y * lax.rsqrt(jnp.mean(y * y, axis=-1, keepdims=True) + NORM_EPS) * gf_ref[...]

        @pl.when(is_sample)
        def _():
            os_ref[...] = y

        @pl.when(jnp.logical_not(is_sample))
        def _():
            op_ref[...] = y


def conv_glu_ffn(xp, xs, g, w_up, conv_w, conv_b, w_down, e1, e2, g_final, seq_len, seq_new, tm=512, tf=256):
    Mp, D = xp.shape
    F = w_down.shape[0]
    assert Mp % tm == 0 and F % tf == 0 and seq_len % tm == 0 and tm % seq_new == 0
    assert xs.shape == (tm, D) and e1.shape == (tm, F) and e2.shape == (tm, F)
    nf = F // tf
    sample_tile = Mp // tm
    prow = pl.BlockSpec((tm, D), lambda i, j: (jnp.minimum(i, sample_tile - 1), 0))
    srow = pl.BlockSpec((tm, D), lambda i, j: (0, 0))
    halo = pl.BlockSpec((SUBLANES, D),
                        lambda i, j: (jnp.clip(i * (tm // SUBLANES) - 1, 0, Mp // SUBLANES - 1), 0))
    vec = pl.BlockSpec((1, D), lambda i, j: (0, 0))
    fill = pl.BlockSpec((tm, tf), lambda i, j: (0, jnp.where(i == sample_tile, j, 0)))
    w_up = w_up.astype(BF16)
    return pl.pallas_call(
        functools.partial(_ffn_kernel, seq_tiles=seq_len // tm, sample_tile=sample_tile, seq_new=seq_new,
                          final_norm=g_final is not None),
        grid=(sample_tile + 1, nf),
        in_specs=[prow, srow, halo, vec,
                  pl.BlockSpec((D, tf), lambda i, j: (0, j)),
                  pl.BlockSpec((D, tf), lambda i, j: (0, j + nf)),
                  pl.BlockSpec((CONV_W, tf), lambda i, j: (0, j)),
                  pl.BlockSpec((1, tf), lambda i, j: (0, j)),
                  pl.BlockSpec((tf, D), lambda i, j: (j, 0)),
                  fill, fill, vec],
        out_specs=[prow, srow],
        out_shape=[jax.ShapeDtypeStruct((Mp, D), F32), jax.ShapeDtypeStruct((tm, D), F32)],
        scratch_shapes=[pltpu.VMEM((tm, D), BF16), pltpu.VMEM((SUBLANES, D), F32), pltpu.VMEM((tm, D), F32)],
        compiler_params=_params("arbitrary", "arbitrary"),
        name="conv_glu_ffn",
    )(xp, xs, xp, g.reshape(1, D), w_up, w_up, conv_w, conv_b.reshape(1, F), w_down.astype(BF16), e1, e2,
      (g if g_final is None else g_final).reshape(1, D))


def _rmsnorm(x, g):
    return x * lax.rsqrt(jnp.mean(x * x, axis=-1, keepdims=True) + NORM_EPS) * g


def _rope_angles(pos):
    half = HEAD_DIM // 2
    inv = jnp.exp(-math.log(ROPE_THETA) * jnp.arange(half, dtype=F32) * 2.0 / HEAD_DIM)
    return pos.astype(F32)[:, None] * inv[None, :]


def _rope_tables(pos):
    ang = _rope_angles(pos)
    cos, sin = jnp.cos(ang), jnp.sin(ang)
    return jnp.tile(cos, (1, LANES // cos.shape[1])), jnp.tile(jnp.concatenate([-sin, sin], axis=1), (1, 2))


def _rope(x, pos):
    half = HEAD_DIM // 2
    ang = _rope_angles(pos)
    cos = jnp.cos(ang)[:, None, None, :]
    sin = jnp.sin(ang)[:, None, None, :]
    x1, x2 = x[..., :half], x[..., half:]
    return jnp.concatenate([x1 * cos - x2 * sin, x2 * cos + x1 * sin], axis=-1)


def kernel(x_prompt, x_sample, state_rwkv_shift, state_rwkv_wkv, cache_k_w128, cache_v_w128, cache_k_w512, cache_v_w512, cache_k_w2048, cache_v_w2048, state_ffn_conv, norm_mix, norm_ffn, norm_final, rwkv_mu, rwkv_w_rkv, rwkv_w0, rwkv_w1, rwkv_w2, rwkv_a0, rwkv_a1, rwkv_a2, rwkv_g1, rwkv_g2, rwkv_k_k, rwkv_k_a, rwkv_r_k, rwkv_ln_w, rwkv_ln_b, rwkv_w_o, attn_w_in, attn_w_o, ffn_w_up, ffn_conv_w, ffn_conv_b, ffn_w_down):
    Bp, Tp, D = x_prompt.shape
    Bs, Ts, _ = x_sample.shape
    Mp, Ms = Bp * Tp, Bs * Ts
    assert Ts >= CONV_W - 1 and Tp >= CONV_W - 1 and CONV_W == 3
    xp = x_prompt.reshape(Mp, D)
    xs = x_sample.reshape(Ms, D)
    caches_k = (cache_k_w128, cache_k_w512, cache_k_w2048)
    caches_v = (cache_v_w128, cache_v_w512, cache_v_w2048)

    p_shift, p_wkv, s_shift, s_wkv = [], [], [], []
    p_k = [[] for _ in range(N_GROUPS)]
    p_v = [[] for _ in range(N_GROUPS)]
    s_k = [[] for _ in range(N_GROUPS)]
    s_v = [[] for _ in range(N_GROUPS)]
    p_conv, s_conv = [], []

    def time_mix(i, li, x, carried, s0, B, T):
        vec = jnp.stack([rwkv_w0[li], rwkv_a0[li], rwkv_k_k[li], rwkv_k_a[li], rwkv_r_k[li].reshape(D)])
        a, wr, w, b, k, v, gate, sc = rwkv_prep(
            x, x if carried is None else carried, norm_mix[i], rwkv_mu[li], rwkv_w_rkv[li], rwkv_w1[li], rwkv_w2[li],
            rwkv_a1[li], rwkv_a2[li], rwkv_g1[li], rwkv_g2[li], vec, T, 0 if carried is None else T)
        if carried is None:
            y, st = wkv_scan_tokens([t.reshape(B, T, D) for t in (a, wr, w, b, k, v)], sc.reshape(B, T, LANES))
            y, st = y.reshape(B * T, D), _state_out(st, B)
        else:
            y, st = _wkv(a, wr, w, b, k, v, sc[:, :N_HEADS], sc[:, N_HEADS:2 * N_HEADS], s0, B, T, T)
        return rwkv_post(y, v, gate, sc, x, rwkv_w_o[li], rwkv_ln_w[li], rwkv_ln_b[li]), st

    depth = norm_mix.shape[0]
    for i in range(depth):
        li = i // 2
        if i % 2 == 0:
            p_shift.append(_rmsnorm(xp.reshape(Bp, Tp, D)[:, -1], norm_mix[i]))
            s_shift.append(_rmsnorm(xs.reshape(Bs, Ts, D)[:, -1], norm_mix[i]))
            xp, stp = time_mix(i, li, xp, None, None, Bp, Tp)
            xs, sts = time_mix(i, li, xs, jnp.repeat(state_rwkv_shift[li], Ts, axis=0), state_rwkv_wkv[li], Bs, Ts)
            p_wkv.append(stp)
            s_wkv.append(sts)
        else:
            hp = _rmsnorm(xp, norm_mix[i])
            hs = _rmsnorm(xs, norm_mix[i])
            assert len(caches_k[0].shape) == 5 and caches_k[0].shape[0] == 1 and li == 0
            hb = hp.astype(BF16)
            outs, lses = [], []
            for g in range(N_GROUPS):
                dil = DILATIONS[g]
                L = Tp // dil
                keep = min(WINDOWS[g], Tp)
                assert Tp % (dil * BAND) == 0 and keep % dil == 0
                classes = lambda t: t.reshape(Bp, L, dil, -1).transpose(0, 2, 1, 3)
                tokens = lambda t: t.reshape(Bp, dil, L, -1).transpose(0, 2, 1, 3)
                hg = classes(hb).reshape(Mp, D)
                qkv_g = mm(hg, attn_w_in[li][:, g * 3 * D:(g + 1) * 3 * D]).reshape(Bp * dil, L, 3 * D)
                pos = (jnp.arange(dil)[:, None] + dil * jnp.arange(L)[None, :]).reshape(-1)
                o, lse, kr = band_attn(qkv_g, *_rope_tables(pos), dil)
                outs.append(tokens(o).reshape(Bp, Tp, D))
                lses.append(tokens(lse).reshape(Bp, Tp, D))
                tail = lambda t: t.reshape(Bp, dil, L, D)[:, :, L - keep // dil:].transpose(0, 2, 1, 3).reshape(
                    Bp, keep, N_HEADS, HEAD_DIM)
                p_k[g].append(tail(kr))
                p_v[g].append(tail(qkv_g[..., 2 * D:]))
            wts = jax.nn.softmax(jnp.stack(lses, axis=0), axis=0)
            op = sum(wts[g] * outs[g] for g in range(N_GROUPS))
            qkv_s = mm(hs.astype(BF16), attn_w_in[li]).reshape(Bs, Ts, N_GROUPS, 3, N_HEADS, HEAD_DIM)
            pos_s = PAST_LEN + jnp.arange(Ts)
            q_s = _rope(qkv_s[:, :, :, 0], pos_s)
            k_s = _rope(qkv_s[:, :, :, 1], pos_s)
            v_s = qkv_s[:, :, :, 2]
            rows8 = lambda t: jnp.pad(t.transpose(0, 2, 3, 1, 4), ((0, 0),) * 3 + ((0, SUBLANES - Ts), (0, 0)))
            osmp = sample_attn(rows8(q_s), rows8(k_s), rows8(v_s), caches_k, caches_v, Ts)
            osmp = osmp[:, :, :Ts].transpose(0, 2, 1, 3).reshape(Ms, D)
            for g in range(N_GROUPS):
                s_k[g].append(k_s[:, :, g])
                s_v[g].append(v_s[:, :, g])
            xp = xp + mm(op.reshape(Mp, D).astype(BF16), attn_w_o[li])
            xs = xs + mm(osmp, attn_w_o[li])

        buf = state_ffn_conv[i]
        first = (jnp.arange(Ms) % Ts == 0)[:, None]
        e1 = jnp.repeat(buf[:, 1], Ts, axis=0)
        e2 = jnp.where(first, jnp.repeat(buf[:, 0], Ts, axis=0), e1)
        tail = jnp.concatenate([xp.reshape(Bp, Tp, D)[:, Tp - 2:].reshape(Bp * 2, D),
                                xs.reshape(Bs, Ts, D)[:, Ts - 2:].reshape(Bs * 2, D)], axis=0)
        gate_tail = mm(_rmsnorm(tail, norm_ffn[i]), ffn_w_up[i][:, :D_FF])
        p_conv.append(gate_tail[:Bp * 2].reshape(Bp, 2, D_FF))
        s_conv.append(gate_tail[Bp * 2:].reshape(Bs, 2, D_FF))
        xp, xs = conv_glu_ffn(xp, xs, norm_ffn[i], ffn_w_up[i], ffn_conv_w[i], ffn_conv_b[i], ffn_w_down[i],
                              e1, e2, norm_final if i == depth - 1 else None, Tp, Ts)

    y_prompt = xp.reshape(Bp, Tp, D)
    y_sample = xs.reshape(Bs, Ts, D)
    st = jnp.stack
    return (y_prompt, y_sample,
            st(p_shift), st(p_wkv),
            st(p_k[0]), st(p_v[0]), st(p_k[1]), st(p_v[1]), st(p_k[2]), st(p_v[2]), st(p_conv),
            st(s_shift), st(s_wkv),
            st(s_k[0]), st(s_v[0]), st(s_k[1]), st(s_v[1]), st(s_k[2]), st(s_v[2]), st(s_conv))
```

```python
import functools
import math

import jax
import jax.numpy as jnp
from jax import lax
from jax.experimental import pallas as pl
from jax.experimental.pallas import tpu as pltpu

F32 = jnp.float32
BF16 = jnp.bfloat16

D_MODEL = 1024
HEAD_DIM = 64
N_HEADS = 16
D_FF = 2816
CONV_W = 3
N_GROUPS = 3
WINDOWS = (128, 512, 2048)
DILATIONS = (1, 4, 16)
BAND = 128
GN_EPS = 64e-5
NORM_EPS = 1e-6
ROPE_THETA = 10000.0
PAST_LEN = 2048
LANES = 128
SUBLANES = 8
VMEM_LIMIT = 56 * 1024 * 1024


def _params(*sem):
    return pltpu.CompilerParams(dimension_semantics=sem, vmem_limit_bytes=VMEM_LIMIT)


def _mm_kernel(x_ref, w_ref, o_ref):
    o_ref[...] = jnp.dot(x_ref[...].astype(BF16), w_ref[...],
                         preferred_element_type=F32).astype(o_ref.dtype)


def mm(x, w, tm=1024, tn=1024):
    M, K = x.shape
    N = w.shape[1]
    tm = next(t for t in (tm, tm // 2, M) if M % t == 0)
    tn = next(t for t in (tn, tn // 2, N) if N % t == 0)
    return pl.pallas_call(
        _mm_kernel,
        grid=(M // tm, N // tn),
        in_specs=[pl.BlockSpec((tm, K), lambda i, j: (i, 0)),
                  pl.BlockSpec((K, tn), lambda i, j: (0, j))],
        out_specs=pl.BlockSpec((tm, tn), lambda i, j: (i, j)),
        out_shape=jax.ShapeDtypeStruct((M, N), F32),
        compiler_params=_params("parallel", "parallel"),
        name="mm",
    )(x, w.astype(BF16))


VALUE_ROWS = 4


def _wkv_state_kernel(a_ref, wr_ref, w_ref, b_ref, k_ref, v_ref, br_ref, kr_ref, s_ref, y_ref, so_ref, *, steps):
    def rows(c, carry):
        v0 = pl.multiple_of(c * VALUE_ROWS, VALUE_ROWS)
        s = s_ref[0, pl.ds(v0, VALUE_ROWS)]
        for t in range(steps):
            sa = jnp.sum(s * a_ref[t, 0][None], axis=1, keepdims=True)
            yy = jnp.sum(s * wr_ref[t, 0][None], axis=1, keepdims=True)
            v = v_ref[t, 0, pl.ds(v0, VALUE_ROWS), :][:, None, :]
            y = yy + sa * br_ref[t, 0][None] + v * kr_ref[t, 0][None]
            y_ref[t, 0, pl.ds(v0, VALUE_ROWS), :] = y[:, 0, :]
            s = s * w_ref[t, 0][None] + sa * b_ref[t, 0][None] + v * k_ref[t, 0][None]
        so_ref[0, pl.ds(v0, VALUE_ROWS)] = s
        return carry

    lax.fori_loop(0, HEAD_DIM // VALUE_ROWS, rows, 0)


def wkv_scan_state(ops, br, kr, s0):
    T, H, dh, B = ops[0].shape
    op_spec = pl.BlockSpec((T, 1, dh, B), lambda h: (0, h, 0, 0))
    sc_spec = pl.BlockSpec((T, 1, 1, B), lambda h: (0, h, 0, 0))
    st_spec = pl.BlockSpec((1, dh, dh, B), lambda h: (h, 0, 0, 0))
    return pl.pallas_call(
        functools.partial(_wkv_state_kernel, steps=T),
        grid=(H,),
        in_specs=[op_spec] * 6 + [sc_spec, sc_spec, st_spec],
        out_specs=[op_spec, st_spec],
        out_shape=[jax.ShapeDtypeStruct((T, H, dh, B), F32), jax.ShapeDtypeStruct(s0.shape, F32)],
        compiler_params=_params("parallel"),
        name="wkv_scan_state",
    )(*ops, br, kr, s0)


CHUNK = LANES
KEYS_PER_TRIP = 16


def _wkv_chunk_kernel(a_ref, wr_ref, w_ref, b_ref, k_ref, v_ref, sc_ref, y_ref, st_ref,
                      state, zrow, zv, zs, zy, xt):
    c = pl.program_id(0)
    nb = a_ref.shape[0]
    rows_bh = nb * N_HEADS

    @pl.when(c == 0)
    def _():
        state[...] = jnp.zeros_like(state)

    def to_channel_major(ref):
        def body(b, carry):
            for cb in range(ref.shape[2] // LANES):
                xt[b, cb * LANES:(cb + 1) * LANES, :] = ref[b, :, cb * LANES:(cb + 1) * LANES].T
            return carry
        lax.fori_loop(0, nb, body, 0)

    def lanes_tile(first, second, stride):
        top = xt[:, pl.ds(first, N_HEADS, stride=stride), :].reshape(rows_bh, CHUNK)
        bot = xt[:, pl.ds(second, N_HEADS, stride=stride), :].reshape(rows_bh, CHUNK)
        return jnp.concatenate([top, bot], axis=0).T

    def grouped(n, body):
        def trip(g, carry):
            for u in range(SUBLANES):
                body(g * SUBLANES + u)
            return carry
        lax.fori_loop(0, n // SUBLANES, trip, 0)

    for op, ref in enumerate((a_ref, wr_ref, w_ref, b_ref, k_ref)):
        to_channel_major(ref)

        def fill(k, op=op):
            zrow[op, k] = lanes_tile(k, k, HEAD_DIM)
        grouped(HEAD_DIM, fill)

    to_channel_major(v_ref)

    def fill_v(v_lo):
        zv[pl.ds(pl.multiple_of(v_lo * CHUNK, CHUNK), CHUNK), :] = lanes_tile(v_lo, HEAD_DIM // 2 + v_lo, HEAD_DIM)
    grouped(HEAD_DIM // 2, fill_v)

    to_channel_major(sc_ref)
    zs[0] = lanes_tile(0, 0, 1)
    zs[1] = lanes_tile(N_HEADS, N_HEADS, 1)

    def row(op, k, t):
        return jnp.broadcast_to(zrow[op, k, pl.ds(t, 1), :], (SUBLANES, LANES))[None]

    def step(t, carry):
        def reduce_keys(g, acc):
            sa, yy = acc
            for u in range(KEYS_PER_TRIP):
                k = g * KEYS_PER_TRIP + u
                sk = state[k]
                sa = sa + sk * row(0, k, t)
                yy = yy + sk * row(1, k, t)
            return sa, yy
        zero = jnp.zeros((4, SUBLANES, LANES), F32)
        sa, yy = lax.fori_loop(0, HEAD_DIM // KEYS_PER_TRIP, reduce_keys, (zero, zero))
        v = zv[pl.ds(t, HEAD_DIM // 2, stride=CHUNK), :].reshape(4, SUBLANES, LANES)
        br = jnp.broadcast_to(zs[0, pl.ds(t, 1), :], (SUBLANES, LANES))[None]
        kr = jnp.broadcast_to(zs[1, pl.ds(t, 1), :], (SUBLANES, LANES))[None]
        zy[pl.ds(t, HEAD_DIM // 2, stride=CHUNK), :] = (yy + sa * br + v * kr).reshape(HEAD_DIM // 2, LANES)

        def update_keys(g, carry):
            for u in range(KEYS_PER_TRIP):
                k = g * KEYS_PER_TRIP + u
                state[k] = state[k] * row(2, k, t) + sa * row(3, k, t) + v * row(4, k, t)
            return carry
        lax.fori_loop(0, HEAD_DIM // KEYS_PER_TRIP, update_keys, 0)
        return carry

    lax.fori_loop(0, CHUNK, step, 0)

    def drain(v_lo):
        tile = zy[pl.ds(pl.multiple_of(v_lo * CHUNK, CHUNK), CHUNK), :].T
        for v_hi in range(2):
            xt[:, pl.ds(v_hi * (HEAD_DIM // 2) + v_lo, N_HEADS, stride=HEAD_DIM), :] = (
                tile[v_hi * rows_bh:(v_hi + 1) * rows_bh].reshape(nb, N_HEADS, CHUNK))
    grouped(HEAD_DIM // 2, drain)

    def to_token_major(b, carry):
        for cb in range(y_ref.shape[2] // LANES):
            y_ref[b, :, cb * LANES:(cb + 1) * LANES] = xt[b, cb * LANES:(cb + 1) * LANES, :].T
        return carry
    lax.fori_loop(0, nb, to_token_major, 0)

    @pl.when(c == pl.num_programs(0) - 1)
    def _():
        st_ref[...] = state[...]


def wkv_scan_tokens(ops, sc):
    B, T, D = ops[0].shape
    assert B * N_HEADS * 2 == LANES and T % CHUNK == 0 and D == N_HEADS * HEAD_DIM
    once = pl.Buffered(1)
    spec = pl.BlockSpec((B, CHUNK, D), lambda c: (0, c, 0), pipeline_mode=once)
    st_shape = (HEAD_DIM, 4, SUBLANES, LANES)
    return pl.pallas_call(
        _wkv_chunk_kernel,
        grid=(T // CHUNK,),
        in_specs=[spec] * 6 + [pl.BlockSpec((B, CHUNK, LANES), lambda c: (0, c, 0), pipeline_mode=once)],
        out_specs=[spec, pl.BlockSpec(st_shape, lambda c: (0, 0, 0, 0))],
        out_shape=[jax.ShapeDtypeStruct((B, T, D), F32), jax.ShapeDtypeStruct(st_shape, F32)],
        scratch_shapes=[pltpu.VMEM(st_shape, F32),
                        pltpu.VMEM((5, HEAD_DIM, CHUNK, LANES), F32),
                        pltpu.VMEM((HEAD_DIM // 2 * CHUNK, LANES), F32),
                        pltpu.VMEM((2, CHUNK, LANES), F32),
                        pltpu.VMEM((HEAD_DIM // 2 * CHUNK, LANES), F32),
                        pltpu.VMEM((B, D, CHUNK), F32)],
        compiler_params=_params("arbitrary"),
        name="wkv_scan_tokens",
    )(*ops, sc)


def _state_out(s, B):
    nl = B * N_HEADS // 64
    s = s.reshape(HEAD_DIM, 32, nl, 2, 64).transpose(2, 4, 3, 1, 0)
    return s.reshape(B, N_HEADS, HEAD_DIM, HEAD_DIM)


def _wkv_carried(a, wr, w, b, k, v, br, kr, s0, B, T):
    lanes_b = lambda t, n: t.reshape(B, T, N_HEADS, n).transpose(1, 2, 3, 0)
    y, st = wkv_scan_state([lanes_b(t, HEAD_DIM) for t in (a, wr, w, b, k, v)], lanes_b(br, 1), lanes_b(kr, 1),
                           jnp.transpose(s0, (1, 2, 3, 0)))
    return y.transpose(3, 0, 1, 2).reshape(B * T, D_MODEL), jnp.transpose(st, (3, 0, 1, 2))


def _dot_split(x, m):
    hi = x.astype(BF16)
    lo = (x - hi.astype(F32)).astype(BF16)
    return jnp.dot(hi, m, preferred_element_type=F32) + jnp.dot(lo, m, preferred_element_type=F32)


def _head_matrices():
    lane_head = jnp.arange(D_MODEL)[:, None] // HEAD_DIM
    col = jnp.arange(LANES)[None, :]
    seg = jnp.stack([(col == lane_head + N_HEADS * j) for j in range(3)]).astype(BF16)
    return seg, jnp.swapaxes(seg, 1, 2)


def _rwkv_prep_kernel(x_ref, halo_ref, e_ref, g_ref, mu_ref, wrkv_ref, w1_ref, w2_ref, a1_ref, a2_ref, g1_ref,
                      g2_ref, vec_ref, seg_ref, spread_ref,
                      a_out, wr_out, w_out, b_out, k_out, v_out, gate_out, sc_out, *, seq_tiles, seq_new):
    i = pl.program_id(0)

    def norm(x):
        return x * lax.rsqrt(jnp.mean(x * x, axis=-1, keepdims=True) + NORM_EPS) * g_ref[...]

    h = norm(x_ref[...])
    row = lax.broadcasted_iota(jnp.int32, h.shape, 0)
    shifted = pltpu.roll(h, 1, axis=0)
    if seq_new:
        prev = jnp.where(row % seq_new == 0, e_ref[...], shifted)
    else:
        before = jnp.where(i % seq_tiles == 0, 0.0, norm(halo_ref[...])[SUBLANES - 1:])
        prev = jnp.where(row == 0, before, shifted)
    xx = prev - h
    mix = lambda n: (h + xx * mu_ref[n:n + 1]).astype(BF16)
    proj = lambda x, w: jnp.dot(x, w, preferred_element_type=F32)
    w0, a0, k_k, k_a, r_k = (vec_ref[n:n + 1] for n in range(5))
    r = proj(mix(0), wrkv_ref[0])
    k = proj(mix(1), wrkv_ref[1])
    v = proj(mix(2), wrkv_ref[2])
    z = -(w0 + proj(jnp.tanh(proj(mix(3), w1_ref[...])).astype(BF16), w2_ref[...]))
    softplus = jnp.maximum(z, 0.0) + jnp.log(1.0 + jnp.exp(-jnp.abs(z)))
    w = jnp.exp(-jnp.exp(-softplus - 0.5))
    a = jax.nn.sigmoid(a0 + proj(proj(mix(4), a1_ref[...]).astype(BF16), a2_ref[...]))
    gate_out[...] = proj(jax.nn.sigmoid(proj(mix(5), g1_ref[...])).astype(BF16), g2_ref[...])
    kk = k * k_k
    norm2 = _dot_split(_dot_split(kk * kk, seg_ref[0]), spread_ref[0])
    kk = kk / jnp.maximum(jnp.sqrt(norm2), 1e-12)
    k = k * (1.0 + (a - 1.0) * k_a)
    b = kk * a
    a_out[...] = -kk
    wr_out[...] = w * r
    w_out[...] = w
    b_out[...] = b
    k_out[...] = k
    v_out[...] = v
    sc_out[...] = (_dot_split(b * r, seg_ref[0]) + _dot_split(k * r, seg_ref[1])
                   + _dot_split(r * k * r_k, seg_ref[2]))


def rwkv_prep(x, e, g, mu, w_rkv, w1, w2, a1, a2, g1, g2, vec, seq_len, seq_new, tm=256):
    M, D = x.shape
    assert M % tm == 0 and (tm % seq_new == 0 if seq_new else seq_len % tm == 0)
    seg, spread = _head_matrices()
    once = pl.Buffered(1)
    row = pl.BlockSpec((tm, D), lambda i: (i, 0))
    halo = pl.BlockSpec((SUBLANES, D), lambda i: (jnp.maximum(i * (tm // SUBLANES) - 1, 0), 0))
    full = lambda t: pl.BlockSpec(t.shape, lambda i: (0,) * t.ndim, pipeline_mode=once)
    consts = [g.reshape(1, D), mu] + [t.astype(BF16) for t in (w_rkv, w1, w2, a1, a2, g1, g2)] + [vec, seg, spread]
    out = jax.ShapeDtypeStruct((M, D), F32)
    return pl.pallas_call(
        functools.partial(_rwkv_prep_kernel, seq_tiles=max(seq_len // tm, 1), seq_new=seq_new),
        grid=(M // tm,),
        in_specs=[row, halo, row] + [full(t) for t in consts],
        out_specs=[row] * 7 + [pl.BlockSpec((tm, LANES), lambda i: (i, 0))],
        out_shape=[out] * 7 + [jax.ShapeDtypeStruct((M, LANES), F32)],
        compiler_params=_params("parallel"),
        name="rwkv_prep",
    )(x, x, e, *consts)


def _rwkv_post_kernel(y_ref, v_ref, gate_ref, sc_ref, x_ref, wo_ref, lnw_ref, lnb_ref, seg_ref, spread_ref, o_ref):
    y = y_ref[...]
    inv = 1.0 / HEAD_DIM
    mean = _dot_split(_dot_split(y, seg_ref[0]), spread_ref[0]) * inv
    d = y - mean
    var = _dot_split(_dot_split(d * d, seg_ref[0]), spread_ref[0]) * inv
    yn = d * lax.rsqrt(var + GN_EPS) * lnw_ref[...] + lnb_ref[...]
    bonus = _dot_split(sc_ref[...], spread_ref[2]) * v_ref[...]
    out = ((yn + bonus) * gate_ref[...]).astype(BF16)
    o_ref[...] = x_ref[...] + jnp.dot(out, wo_ref[...], preferred_element_type=F32)


def rwkv_post(y, v, gate, sc, x, w_o, ln_w, ln_b, tm=512):
    M, D = x.shape
    assert M % tm == 0
    seg, spread = _head_matrices()
    once = pl.Buffered(1)
    row = pl.BlockSpec((tm, D), lambda i: (i, 0))
    full = lambda t: pl.BlockSpec(t.shape, lambda i: (0,) * t.ndim, pipeline_mode=once)
    consts = [w_o.astype(BF16), ln_w.reshape(1, D), ln_b.reshape(1, D), seg, spread]
    return pl.pallas_call(
        _rwkv_post_kernel,
        grid=(M // tm,),
        in_specs=[row, row, row, pl.BlockSpec((tm, LANES), lambda i: (i, 0)), row] + [full(t) for t in consts],
        out_specs=row,
        out_shape=jax.ShapeDtypeStruct((M, D), F32),
        compiler_params=_params("parallel"),
        name="rwkv_post",
    )(y, v, gate, sc, x, *consts)


def _band_attn_kernel(q_ref, k_ref, v_ref, cos_ref, sin_ref, o_ref, lse_ref, kr_ref, kprev, vprev):
    n = pl.program_id(1)

    @pl.when(n == 0)
    def _():
        kprev[...] = jnp.zeros_like(kprev)
        vprev[...] = jnp.zeros_like(vprev)

    i = lax.broadcasted_iota(jnp.int32, (BAND, 2 * BAND), 0)
    j = lax.broadcasted_iota(jnp.int32, (BAND, 2 * BAND), 1)
    valid = (j >= i) & (j <= i + BAND) & ((n > 0) | (j >= BAND))
    lane = lax.broadcasted_iota(jnp.int32, (BAND, LANES), 1)
    first_head = lane < HEAD_DIM
    first_half = lane % HEAD_DIM < HEAD_DIM // 2
    cos, sin = cos_ref[...], sin_ref[...]

    def rope(x):
        partner = jnp.where(first_half, pltpu.roll(x, LANES - HEAD_DIM // 2, axis=1),
                            pltpu.roll(x, HEAD_DIM // 2, axis=1))
        return x * cos + partner * sin

    for p in range(N_HEADS // 2):
        cs = slice(p * LANES, (p + 1) * LANES)
        q = rope(q_ref[0, :, cs])
        k = rope(k_ref[0, :, cs])
        kr_ref[0, :, cs] = k
        kb = k.astype(BF16)
        vb = v_ref[0, :, cs].astype(BF16)
        kcat = jnp.concatenate([kprev[:, cs], kb], axis=0)
        vcat = jnp.concatenate([vprev[:, cs], vb], axis=0)
        kprev[:, cs] = kb
        vprev[:, cs] = vb
        outs, lses = [], []
        for head_mask in (first_head, jnp.logical_not(first_head)):
            qm = jnp.where(head_mask, q, 0.0).astype(BF16)
            s = lax.dot_general(qm, kcat, (((1,), (1,)), ((), ())), preferred_element_type=F32)
            s = jnp.where(valid, s * (HEAD_DIM ** -0.5), -jnp.inf)
            m = jnp.max(s, axis=-1, keepdims=True)
            e = jnp.exp(s - m)
            den = jnp.sum(e, axis=-1, keepdims=True)
            outs.append(jnp.dot(e.astype(BF16), vcat, preferred_element_type=F32) / den)
            lses.append(jnp.broadcast_to(m + jnp.log(den), (BAND, LANES)))
        o_ref[0, :, cs] = jnp.where(first_head, outs[0], outs[1])
        lse_ref[0, :, cs] = jnp.where(first_head, lses[0], lses[1])


def band_attn(qkv, cos, sin, dil):
    S, L, W = qkv.shape
    D = W // 3
    nb = L // BAND
    col = lambda m: pl.BlockSpec((1, BAND, D), lambda s, n: (s, n, m))
    tab = pl.BlockSpec((BAND, LANES), lambda s, n: ((s % dil) * nb + n, 0))
    out = jax.ShapeDtypeStruct((S, L, D), F32)
    return pl.pallas_call(
        _band_attn_kernel,
        grid=(S, nb),
        in_specs=[col(0), col(1), col(2), tab, tab],
        out_specs=[col(0)] * 3,
        out_shape=[out] * 3,
        scratch_shapes=[pltpu.VMEM((BAND, D), BF16), pltpu.VMEM((BAND, D), BF16)],
        compiler_params=_params("parallel", "arbitrary"),
        name="band_attn",
    )(qkv, qkv, qkv, cos, sin)


def _attn_merge_kernel(*refs, dils):
    n = len(dils)
    o_refs, l_refs = refs[:n], refs[n:2 * n]
    x_ref, wo_ref, out_ref, scr = refs[2 * n:]

    def token_major(ref, dil):
        if dil == 1:
            return ref[0, 0]
        for c in range(dil):
            for j in range(scr.shape[0]):
                scr[j, pl.ds(c, ref.shape[2], stride=dil), :] = ref[0, c, :, j * LANES:(j + 1) * LANES]
        return jnp.concatenate([scr[j] for j in range(scr.shape[0])], axis=1)

    outs = [token_major(r, d) for r, d in zip(o_refs, dils)]
    lses = [token_major(r, d) for r, d in zip(l_refs, dils)]
    top = functools.reduce(jnp.maximum, lses)
    wts = [jnp.exp(l - top) for l in lses]
    merged = sum(w * o for w, o in zip(wts, outs)) / sum(wts)
    out_ref[...] = x_ref[...] + jnp.dot(merged.astype(BF16), wo_ref[...], preferred_element_type=F32)


def attn_merge(outs, lses, dils, x, w_o, seq_len, tm=256):
    M, D = x.shape
    assert seq_len % tm == 0 and all(tm % (d * SUBLANES) == 0 for d in dils)
    per_seq = seq_len // tm
    group = lambda t, d: t.reshape(M // seq_len, d, seq_len // d, D)
    spec = lambda d: pl.BlockSpec((1, d, tm // d, D), lambda i: (i // per_seq, 0, i % per_seq, 0))
    row = pl.BlockSpec((tm, D), lambda i: (i, 0))
    return pl.pallas_call(
        functools.partial(_attn_merge_kernel, dils=tuple(dils)),
        grid=(M // tm,),
        in_specs=[spec(d) for d in dils] * 2 + [row, pl.BlockSpec((D, D), lambda i: (0, 0))],
        out_specs=row,
        out_shape=jax.ShapeDtypeStruct((M, D), F32),
        scratch_shapes=[pltpu.VMEM((D // LANES, tm, LANES), F32)],
        compiler_params=_params("parallel"),
        name="attn_merge",
    )(*[group(t, d) for t, d in zip(outs, dils)], *[group(t, d) for t, d in zip(lses, dils)], x, w_o.astype(BF16))


def _sample_attn_kernel(q_ref, kn_ref, vn_ref, ck0_ref, cv0_ref, ck1_ref, cv1_ref, ck2_ref, cv2_ref,
                        o_ref, *, n_new):
    scale = HEAD_DIM ** -0.5
    caches = ((ck0_ref, cv0_ref), (ck1_ref, cv1_ref), (ck2_ref, cv2_ref))
    jcol = lax.broadcasted_iota(jnp.int32, (SUBLANES, 1), 0)
    masks = []
    for ck_ref, _ in caches:
        rows = ck_ref.shape[-1]
        dil = rows // BAND
        qi = lax.broadcasted_iota(jnp.int32, (SUBLANES, rows), 0)
        ri = lax.broadcasted_iota(jnp.int32, (SUBLANES, rows), 1)
        masks.append(ri >= qi if dil == 1 else ri % dil == qi)

    def head(h, carry):
        sc, sn = [], []
        for g, (ck_ref, _) in enumerate(caches):
            q = q_ref[0, g, h]
            s = jnp.dot(q.astype(BF16), ck_ref[0, 0, h].astype(BF16), preferred_element_type=F32) * scale
            sc.append(jnp.where(masks[g], s, -jnp.inf))
            kn = kn_ref[0, g, h]
            for i in range(n_new):
                si = jnp.sum(q * kn[i:i + 1, :], axis=-1, keepdims=True) * scale
                sn.append(jnp.where(jcol >= i if g == 0 else jcol == i, si, -jnp.inf))
        m = functools.reduce(jnp.maximum, [jnp.max(s, axis=-1, keepdims=True) for s in sc] + sn)
        den = jnp.zeros((SUBLANES, 1), F32)
        acc = jnp.zeros((SUBLANES, HEAD_DIM), F32)
        for g, (_, cv_ref) in enumerate(caches):
            e = jnp.exp(sc[g] - m)
            den = den + jnp.sum(e, axis=-1, keepdims=True)
            acc = acc + lax.dot_general(e.astype(BF16), cv_ref[0, 0, h].astype(BF16),
                                        (((1,), (1,)), ((), ())), preferred_element_type=F32)
            vn = vn_ref[0, g, h]
            for i in range(n_new):
                en = jnp.exp(sn[g * n_new + i] - m)
                den = den + en
                acc = acc + en * vn[i:i + 1, :]
        o_ref[0, h] = acc / den
        return carry

    lax.fori_loop(0, N_HEADS, head, 0)


def sample_attn(q, kn, vn, caches_k, caches_v, n_new):
    B = q.shape[0]
    new_spec = pl.BlockSpec((1, N_GROUPS, N_HEADS, SUBLANES, HEAD_DIM), lambda b: (b, 0, 0, 0, 0))
    specs = [new_spec] * 3
    args = [q, kn, vn]
    for g in range(N_GROUPS):
        rows = caches_k[g].shape[2]
        assert rows == BAND * DILATIONS[g] and n_new <= SUBLANES and (g == 0 or n_new <= DILATIONS[g])
        spec = pl.BlockSpec((1, 1, N_HEADS, HEAD_DIM, rows), lambda b: (0, b, 0, 0, 0))
        specs += [spec, spec]
        args += [jnp.transpose(caches_k[g], (0, 1, 3, 4, 2)), jnp.transpose(caches_v[g], (0, 1, 3, 4, 2))]
    return pl.pallas_call(
        functools.partial(_sample_attn_kernel, n_new=n_new),
        grid=(B,),
        in_specs=specs,
        out_specs=pl.BlockSpec((1, N_HEADS, SUBLANES, HEAD_DIM), lambda b: (b, 0, 0, 0)),
        out_shape=jax.ShapeDtypeStruct((B, N_HEADS, SUBLANES, HEAD_DIM), F32),
        compiler_params=_params("parallel"),
        name="sample_attn",
    )(*args)


def _ffn_kernel(xp_ref, xs_ref, halo_ref, g_ref, wg_ref, wv_ref, cw_ref, cb_ref, wd_ref, e1_ref, e2_ref, gf_ref,
                op_ref, os_ref, hn_scr, hh_scr, acc_scr, *, seq_tiles, sample_tile, seq_new, final_norm):
    i = pl.program_id(0)
    j = pl.program_id(1)
    is_sample = i == sample_tile

    def norm(x):
        return (x * lax.rsqrt(jnp.mean(x * x, axis=-1, keepdims=True) + NORM_EPS) * g_ref[...]).astype(BF16)

    def x_tile():
        return jnp.where(is_sample, xs_ref[...], xp_ref[...])

    @pl.when(j == 0)
    def _():
        hn_scr[...] = norm(x_tile())
        hh_scr[...] = norm(halo_ref[...]).astype(F32)
        acc_scr[...] = jnp.zeros_like(acc_scr)

    hn = hn_scr[...]
    gate = jnp.dot(hn, wg_ref[...], preferred_element_type=F32)
    val = jnp.dot(hn, wv_ref[...], preferred_element_type=F32)
    gh = jnp.dot(hh_scr[...].astype(BF16), wg_ref[...], preferred_element_type=F32)
    gh = jnp.where(i % seq_tiles == 0, 0.0, gh)
    row = lax.broadcasted_iota(jnp.int32, gate.shape, 0)
    roll1 = pltpu.roll(gate, 1, axis=0)
    roll2 = pltpu.roll(gate, 2, axis=0)
    pos = row % seq_new
    prev1 = jnp.where(is_sample,
                      jnp.where(pos == 0, e1_ref[...], roll1),
                      jnp.where(row == 0, gh[7:8], roll1))
    prev2 = jnp.where(is_sample,
                      jnp.where(pos < 2, e2_ref[...], roll2),
                      jnp.where(row == 0, gh[6:7], jnp.where(row == 1, gh[7:8], roll2)))
    conv = cb_ref[...] + prev2 * cw_ref[0:1] + prev1 * cw_ref[1:2] + gate * cw_ref[2:3]
    act = (conv * jax.nn.sigmoid(conv) * val).astype(BF16)
    acc_scr[...] += jnp.dot(act, wd_ref[...], preferred_element_type=F32)

    @pl.when(j == pl.num_programs(1) - 1)
    def _():
        y = x_tile() + acc_scr[...]
        if final_norm:
            y = y * lax.rsqrt(jnp.mean(y * y, axis=-1, keepdims=True) + NORM_EPS) * gf_ref[...]

        @pl.when(is_sample)
        def _():
            os_ref[...] = y

        @pl.when(jnp.logical_not(is_sample))
        def _():
            op_ref[...] = y


def conv_glu_ffn(xp, xs, g, w_up, conv_w, conv_b, w_down, e1, e2, g_final, seq_len, seq_new, tm=512, tf=256):
    Mp, D = xp.shape
    F = w_down.shape[0]
    assert Mp % tm == 0 and F % tf == 0 and seq_len % tm == 0 and tm % seq_new == 0
    assert xs.shape == (tm, D) and e1.shape == (tm, F) and e2.shape == (tm, F)
    nf = F // tf
    sample_tile = Mp // tm
    prow = pl.BlockSpec((tm, D), lambda i, j: (jnp.minimum(i, sample_tile - 1), 0))
    srow = pl.BlockSpec((tm, D), lambda i, j: (0, 0))
    halo = pl.BlockSpec((SUBLANES, D),
                        lambda i, j: (jnp.clip(i * (tm // SUBLANES) - 1, 0, Mp // SUBLANES - 1), 0))
    vec = pl.BlockSpec((1, D), lambda i, j: (0, 0))
    fill = pl.BlockSpec((tm, tf), lambda i, j: (0, jnp.where(i == sample_tile, j, 0)))
    w_up = w_up.astype(BF16)
    return pl.pallas_call(
        functools.partial(_ffn_kernel, seq_tiles=seq_len // tm, sample_tile=sample_tile, seq_new=seq_new,
                          final_norm=g_final is not None),
        grid=(sample_tile + 1, nf),
        in_specs=[prow, srow, halo, vec,
                  pl.BlockSpec((D, tf), lambda i, j: (0, j)),
                  pl.BlockSpec((D, tf), lambda i, j: (0, j + nf)),
                  pl.BlockSpec((CONV_W, tf), lambda i, j: (0, j)),
                  pl.BlockSpec((1, tf), lambda i, j: (0, j)),
                  pl.BlockSpec((tf, D), lambda i, j: (j, 0)),
                  fill, fill, vec],
        out_specs=[prow, srow],
        out_shape=[jax.ShapeDtypeStruct((Mp, D), F32), jax.ShapeDtypeStruct((tm, D), F32)],
        scratch_shapes=[pltpu.VMEM((tm, D), BF16), pltpu.VMEM((SUBLANES, D), F32), pltpu.VMEM((tm, D), F32)],
        compiler_params=_params("arbitrary", "arbitrary"),
        name="conv_glu_ffn",
    )(xp, xs, xp, g.reshape(1, D), w_up, w_up, conv_w, conv_b.reshape(1, F), w_down.astype(BF16), e1, e2,
      (g if g_final is None else g_final).reshape(1, D))


def _rmsnorm(x, g):
    return x * lax.rsqrt(jnp.mean(x * x, axis=-1, keepdims=True) + NORM_EPS) * g


def _rope_angles(pos):
    half = HEAD_DIM // 2
    inv = jnp.exp(-math.log(ROPE_THETA) * jnp.arange(half, dtype=F32) * 2.0 / HEAD_DIM)
    return pos.astype(F32)[:, None] * inv[None, :]


def _rope_tables(pos):
    ang = _rope_angles(pos)
    cos, sin = jnp.cos(ang), jnp.sin(ang)
    return jnp.tile(cos, (1, LANES // cos.shape[1])), jnp.tile(jnp.concatenate([-sin, sin], axis=1), (1, 2))


def _rope(x, pos):
    half = HEAD_DIM // 2
    ang = _rope_angles(pos)
    cos = jnp.cos(ang)[:, None, None, :]
    sin = jnp.sin(ang)[:, None, None, :]
    x1, x2 = x[..., :half], x[..., half:]
    return jnp.concatenate([x1 * cos - x2 * sin, x2 * cos + x1 * sin], axis=-1)


def kernel(x_prompt, x_sample, state_rwkv_shift, state_rwkv_wkv, cache_k_w128, cache_v_w128, cache_k_w512, cache_v_w512, cache_k_w2048, cache_v_w2048, state_ffn_conv, norm_mix, norm_ffn, norm_final, rwkv_mu, rwkv_w_rkv, rwkv_w0, rwkv_w1, rwkv_w2, rwkv_a0, rwkv_a1, rwkv_a2, rwkv_g1, rwkv_g2, rwkv_k_k, rwkv_k_a, rwkv_r_k, rwkv_ln_w, rwkv_ln_b, rwkv_w_o, attn_w_in, attn_w_o, ffn_w_up, ffn_conv_w, ffn_conv_b, ffn_w_down):
    Bp, Tp, D = x_prompt.shape
    Bs, Ts, _ = x_sample.shape
    Mp, Ms = Bp * Tp, Bs * Ts
    assert Ts >= CONV_W - 1 and Tp >= CONV_W - 1 and CONV_W == 3
    xp = x_prompt.reshape(Mp, D)
    xs = x_sample.reshape(Ms, D)
    caches_k = (cache_k_w128, cache_k_w512, cache_k_w2048)
    caches_v = (cache_v_w128, cache_v_w512, cache_v_w2048)

    p_shift, p_wkv, s_shift, s_wkv = [], [], [], []
    p_k = [[] for _ in range(N_GROUPS)]
    p_v = [[] for _ in range(N_GROUPS)]
    s_k = [[] for _ in range(N_GROUPS)]
    s_v = [[] for _ in range(N_GROUPS)]
    p_conv, s_conv = [], []

    def time_mix(i, li, x, carried, s0, B, T):
        vec = jnp.stack([rwkv_w0[li], rwkv_a0[li], rwkv_k_k[li], rwkv_k_a[li], rwkv_r_k[li].reshape(D)])
        a, wr, w, b, k, v, gate, sc = rwkv_prep(
            x, x if carried is None else carried, norm_mix[i], rwkv_mu[li], rwkv_w_rkv[li], rwkv_w1[li], rwkv_w2[li],
            rwkv_a1[li], rwkv_a2[li], rwkv_g1[li], rwkv_g2[li], vec, T, 0 if carried is None else T)
        if carried is None:
            y, st = wkv_scan_tokens([t.reshape(B, T, D) for t in (a, wr, w, b, k, v)], sc.reshape(B, T, LANES))
            y, st = y.reshape(B * T, D), _state_out(st, B)
        else:
            y, st = _wkv_carried(a, wr, w, b, k, v, sc[:, :N_HEADS], sc[:, N_HEADS:2 * N_HEADS], s0, B, T)
        return rwkv_post(y, v, gate, sc, x, rwkv_w_o[li], rwkv_ln_w[li], rwkv_ln_b[li]), st

    depth = norm_mix.shape[0]
    for i in range(depth):
        li = i // 2
        if i % 2 == 0:
            p_shift.append(_rmsnorm(xp.reshape(Bp, Tp, D)[:, -1], norm_mix[i]))
            s_shift.append(_rmsnorm(xs.reshape(Bs, Ts, D)[:, -1], norm_mix[i]))
            xp, stp = time_mix(i, li, xp, None, None, Bp, Tp)
            xs, sts = time_mix(i, li, xs, jnp.repeat(state_rwkv_shift[li], Ts, axis=0), state_rwkv_wkv[li], Bs, Ts)
            p_wkv.append(stp)
            s_wkv.append(sts)
        else:
            hp = _rmsnorm(xp, norm_mix[i])
            hs = _rmsnorm(xs, norm_mix[i])
            assert len(caches_k[0].shape) == 5 and caches_k[0].shape[0] == 1 and li == 0
            hb = hp.astype(BF16)
            outs, lses = [], []
            for g in range(N_GROUPS):
                dil = DILATIONS[g]
                L = Tp // dil
                keep = min(WINDOWS[g], Tp)
                assert Tp % (dil * BAND) == 0 and keep % dil == 0
                classes = lambda t: t.reshape(Bp, L, dil, -1).transpose(0, 2, 1, 3)
                hg = classes(hb).reshape(Mp, D)
                qkv_g = mm(hg, attn_w_in[li][:, g * 3 * D:(g + 1) * 3 * D]).reshape(Bp * dil, L, 3 * D)
                pos = (jnp.arange(dil)[:, None] + dil * jnp.arange(L)[None, :]).reshape(-1)
                o, lse, kr = band_attn(qkv_g, *_rope_tables(pos), dil)
                outs.append(o)
                lses.append(lse)
                tail = lambda t: t.reshape(Bp, dil, L, D)[:, :, L - keep // dil:].transpose(0, 2, 1, 3).reshape(
                    Bp, keep, N_HEADS, HEAD_DIM)
                p_k[g].append(tail(kr))
                p_v[g].append(tail(qkv_g[..., 2 * D:]))
            qkv_s = mm(hs.astype(BF16), attn_w_in[li]).reshape(Bs, Ts, N_GROUPS, 3, N_HEADS, HEAD_DIM)
            pos_s = PAST_LEN + jnp.arange(Ts)
            q_s = _rope(qkv_s[:, :, :, 0], pos_s)
            k_s = _rope(qkv_s[:, :, :, 1], pos_s)
            v_s = qkv_s[:, :, :, 2]
            rows8 = lambda t: jnp.pad(t.transpose(0, 2, 3, 1, 4), ((0, 0),) * 3 + ((0, SUBLANES - Ts), (0, 0)))
            osmp = sample_attn(rows8(q_s), rows8(k_s), rows8(v_s), caches_k, caches_v, Ts)
            osmp = osmp[:, :, :Ts].transpose(0, 2, 1, 3).reshape(Ms, D)
            for g in range(N_GROUPS):
                s_k[g].append(k_s[:, :, g])
                s_v[g].append(v_s[:, :, g])
            xp = attn_merge(outs, lses, DILATIONS, xp, attn_w_o[li], Tp)
            xs = xs + mm(osmp, attn_w_o[li])

        buf = state_ffn_conv[i]
        first = (jnp.arange(Ms) % Ts == 0)[:, None]
        e1 = jnp.repeat(buf[:, 1], Ts, axis=0)
        e2 = jnp.where(first, jnp.repeat(buf[:, 0], Ts, axis=0), e1)
        tail = jnp.concatenate([xp.reshape(Bp, Tp, D)[:, Tp - 2:].reshape(Bp * 2, D),
                                xs.reshape(Bs, Ts, D)[:, Ts - 2:].reshape(Bs * 2, D)], axis=0)
        gate_tail = mm(_rmsnorm(tail, norm_ffn[i]), ffn_w_up[i][:, :D_FF])
        p_conv.append(gate_tail[:Bp * 2].reshape(Bp, 2, D_FF))
        s_conv.append(gate_tail[Bp * 2:].reshape(Bs, 2, D_FF))
        xp, xs = conv_glu_ffn(xp, xs, norm_ffn[i], ffn_w_up[i], ffn_conv_w[i], ffn_conv_b[i], ffn_w_down[i],
                              e1, e2, norm_final if i == depth - 1 else None, Tp, Ts)

    y_prompt = xp.reshape(Bp, Tp, D)
    y_sample = xs.reshape(Bs, Ts, D)
    st = jnp.stack
    return (y_prompt, y_sample,
            st(p_shift), st(p_wkv),
            st(p_k[0]), st(p_v[0]), st(p_k[1]), st(p_v[1]), st(p_k[2]), st(p_v[2]), st(p_conv),
            st(s_shift), st(s_wkv),
            st(s_k[0]), st(s_v[0]), st(s_k[1]), st(s_v[1]), st(s_k[2]), st(s_v[2]), st(s_conv))
```

```python
import functools
import math

import jax
import jax.numpy as jnp
from jax import lax
from jax.experimental import pallas as pl
from jax.experimental.pallas import tpu as pltpu

F32 = jnp.float32
BF16 = jnp.bfloat16

D_MODEL = 1024
HEAD_DIM = 64
N_HEADS = 16
D_FF = 2816
CONV_W = 3
N_GROUPS = 3
WINDOWS = (128, 512, 2048)
DILATIONS = (1, 4, 16)
BAND = 128
GN_EPS = 64e-5
NORM_EPS = 1e-6
ROPE_THETA = 10000.0
PAST_LEN = 2048
LANES = 128
SUBLANES = 8
VMEM_LIMIT = 56 * 1024 * 1024


def _params(*sem):
    return pltpu.CompilerParams(dimension_semantics=sem, vmem_limit_bytes=VMEM_LIMIT)


def _mm_kernel(x_ref, w_ref, o_ref):
    o_ref[...] = jnp.dot(x_ref[...].astype(BF16), w_ref[...],
                         preferred_element_type=F32).astype(o_ref.dtype)


def mm(x, w, tm=1024, tn=1024):
    M, K = x.shape
    N = w.shape[1]
    tm = next(t for t in (tm, tm // 2, M) if M % t == 0)
    tn = next(t for t in (tn, tn // 2, N) if N % t == 0)
    return pl.pallas_call(
        _mm_kernel,
        grid=(M // tm, N // tn),
        in_specs=[pl.BlockSpec((tm, K), lambda i, j: (i, 0)),
                  pl.BlockSpec((K, tn), lambda i, j: (0, j))],
        out_specs=pl.BlockSpec((tm, tn), lambda i, j: (i, j)),
        out_shape=jax.ShapeDtypeStruct((M, N), F32),
        compiler_params=_params("parallel", "parallel"),
        name="mm",
    )(x, w.astype(BF16))


VALUE_ROWS = 4


def _wkv_state_kernel(a_ref, wr_ref, w_ref, b_ref, k_ref, v_ref, br_ref, kr_ref, s_ref, y_ref, so_ref, *, steps):
    def rows(c, carry):
        v0 = pl.multiple_of(c * VALUE_ROWS, VALUE_ROWS)
        s = s_ref[0, pl.ds(v0, VALUE_ROWS)]
        for t in range(steps):
            sa = jnp.sum(s * a_ref[t, 0][None], axis=1, keepdims=True)
            yy = jnp.sum(s * wr_ref[t, 0][None], axis=1, keepdims=True)
            v = v_ref[t, 0, pl.ds(v0, VALUE_ROWS), :][:, None, :]
            y = yy + sa * br_ref[t, 0][None] + v * kr_ref[t, 0][None]
            y_ref[t, 0, pl.ds(v0, VALUE_ROWS), :] = y[:, 0, :]
            s = s * w_ref[t, 0][None] + sa * b_ref[t, 0][None] + v * k_ref[t, 0][None]
        so_ref[0, pl.ds(v0, VALUE_ROWS)] = s
        return carry

    lax.fori_loop(0, HEAD_DIM // VALUE_ROWS, rows, 0)


def wkv_scan_state(ops, br, kr, s0):
    T, H, dh, B = ops[0].shape
    op_spec = pl.BlockSpec((T, 1, dh, B), lambda h: (0, h, 0, 0))
    sc_spec = pl.BlockSpec((T, 1, 1, B), lambda h: (0, h, 0, 0))
    st_spec = pl.BlockSpec((1, dh, dh, B), lambda h: (h, 0, 0, 0))
    return pl.pallas_call(
        functools.partial(_wkv_state_kernel, steps=T),
        grid=(H,),
        in_specs=[op_spec] * 6 + [sc_spec, sc_spec, st_spec],
        out_specs=[op_spec, st_spec],
        out_shape=[jax.ShapeDtypeStruct((T, H, dh, B), F32), jax.ShapeDtypeStruct(s0.shape, F32)],
        compiler_params=_params("parallel"),
        name="wkv_scan_state",
    )(*ops, br, kr, s0)


CHUNK = LANES
KEYS_PER_TRIP = 32


def _wkv_chunk_kernel(a_ref, wr_ref, w_ref, b_ref, k_ref, v_ref, sc_ref, y_ref, st_ref,
                      state, zrow, zv, zs, zy, xt):
    c = pl.program_id(0)
    nb = a_ref.shape[0]
    rows_bh = nb * N_HEADS

    @pl.when(c == 0)
    def _():
        state[...] = jnp.zeros_like(state)

    def to_channel_major(ref):
        def body(b, carry):
            for cb in range(ref.shape[2] // LANES):
                xt[b, cb * LANES:(cb + 1) * LANES, :] = ref[b, :, cb * LANES:(cb + 1) * LANES].T
            return carry
        lax.fori_loop(0, nb, body, 0)

    def lanes_tile(first, second, stride):
        top = xt[:, pl.ds(first, N_HEADS, stride=stride), :].reshape(rows_bh, CHUNK)
        bot = xt[:, pl.ds(second, N_HEADS, stride=stride), :].reshape(rows_bh, CHUNK)
        return jnp.concatenate([top, bot], axis=0).T

    def grouped(n, body):
        def trip(g, carry):
            for u in range(SUBLANES):
                body(g * SUBLANES + u)
            return carry
        lax.fori_loop(0, n // SUBLANES, trip, 0)

    for op, ref in enumerate((a_ref, wr_ref, w_ref, b_ref, k_ref)):
        to_channel_major(ref)

        def fill(k, op=op):
            zrow[op, k] = lanes_tile(k, k, HEAD_DIM)
        grouped(HEAD_DIM, fill)

    to_channel_major(v_ref)

    def fill_v(v_lo):
        zv[pl.ds(pl.multiple_of(v_lo * CHUNK, CHUNK), CHUNK), :] = lanes_tile(v_lo, HEAD_DIM // 2 + v_lo, HEAD_DIM)
    grouped(HEAD_DIM // 2, fill_v)

    to_channel_major(sc_ref)
    zs[0] = lanes_tile(0, 0, 1)
    zs[1] = lanes_tile(N_HEADS, N_HEADS, 1)

    def row(op, k, t):
        return jnp.broadcast_to(zrow[op, k, pl.ds(t, 1), :], (SUBLANES, LANES))[None]

    def step(t, carry):
        def reduce_keys(g, acc):
            sa, yy = acc
            for u in range(KEYS_PER_TRIP):
                k = g * KEYS_PER_TRIP + u
                sk = state[k]
                sa = sa + sk * row(0, k, t)
                yy = yy + sk * row(1, k, t)
            return sa, yy
        zero = jnp.zeros((4, SUBLANES, LANES), F32)
        sa, yy = lax.fori_loop(0, HEAD_DIM // KEYS_PER_TRIP, reduce_keys, (zero, zero))
        v = zv[pl.ds(t, HEAD_DIM // 2, stride=CHUNK), :].reshape(4, SUBLANES, LANES)
        br = jnp.broadcast_to(zs[0, pl.ds(t, 1), :], (SUBLANES, LANES))[None]
        kr = jnp.broadcast_to(zs[1, pl.ds(t, 1), :], (SUBLANES, LANES))[None]
        zy[pl.ds(t, HEAD_DIM // 2, stride=CHUNK), :] = (yy + sa * br + v * kr).reshape(HEAD_DIM // 2, LANES)

        def update_keys(g, carry):
            for u in range(KEYS_PER_TRIP):
                k = g * KEYS_PER_TRIP + u
                state[k] = state[k] * row(2, k, t) + sa * row(3, k, t) + v * row(4, k, t)
            return carry
        lax.fori_loop(0, HEAD_DIM // KEYS_PER_TRIP, update_keys, 0)
        return carry

    lax.fori_loop(0, CHUNK, step, 0)

    def drain(v_lo):
        tile = zy[pl.ds(pl.multiple_of(v_lo * CHUNK, CHUNK), CHUNK), :].T
        for v_hi in range(2):
            xt[:, pl.ds(v_hi * (HEAD_DIM // 2) + v_lo, N_HEADS, stride=HEAD_DIM), :] = (
                tile[v_hi * rows_bh:(v_hi + 1) * rows_bh].reshape(nb, N_HEADS, CHUNK))
    grouped(HEAD_DIM // 2, drain)

    def to_token_major(b, carry):
        for cb in range(y_ref.shape[2] // LANES):
            y_ref[b, :, cb * LANES:(cb + 1) * LANES] = xt[b, cb * LANES:(cb + 1) * LANES, :].T
        return carry
    lax.fori_loop(0, nb, to_token_major, 0)

    @pl.when(c == pl.num_programs(0) - 1)
    def _():
        st_ref[...] = state[...]


def wkv_scan_tokens(ops, sc):
    B, T, D = ops[0].shape
    assert B * N_HEADS * 2 == LANES and T % CHUNK == 0 and D == N_HEADS * HEAD_DIM
    once = pl.Buffered(1)
    spec = pl.BlockSpec((B, CHUNK, D), lambda c: (0, c, 0), pipeline_mode=once)
    st_shape = (HEAD_DIM, 4, SUBLANES, LANES)
    return pl.pallas_call(
        _wkv_chunk_kernel,
        grid=(T // CHUNK,),
        in_specs=[spec] * 6 + [pl.BlockSpec((B, CHUNK, LANES), lambda c: (0, c, 0), pipeline_mode=once)],
        out_specs=[spec, pl.BlockSpec(st_shape, lambda c: (0, 0, 0, 0))],
        out_shape=[jax.ShapeDtypeStruct((B, T, D), F32), jax.ShapeDtypeStruct(st_shape, F32)],
        scratch_shapes=[pltpu.VMEM(st_shape, F32),
                        pltpu.VMEM((5, HEAD_DIM, CHUNK, LANES), F32),
                        pltpu.VMEM((HEAD_DIM // 2 * CHUNK, LANES), F32),
                        pltpu.VMEM((2, CHUNK, LANES), F32),
                        pltpu.VMEM((HEAD_DIM // 2 * CHUNK, LANES), F32),
                        pltpu.VMEM((B, D, CHUNK), F32)],
        compiler_params=_params("arbitrary"),
        name="wkv_scan_tokens",
    )(*ops, sc)


def _state_out(s, B):
    nl = B * N_HEADS // 64
    s = s.reshape(HEAD_DIM, 32, nl, 2, 64).transpose(2, 4, 3, 1, 0)
    return s.reshape(B, N_HEADS, HEAD_DIM, HEAD_DIM)


def _wkv_carried(a, wr, w, b, k, v, br, kr, s0, B, T):
    lanes_b = lambda t, n: t.reshape(B, T, N_HEADS, n).transpose(1, 2, 3, 0)
    y, st = wkv_scan_state([lanes_b(t, HEAD_DIM) for t in (a, wr, w, b, k, v)], lanes_b(br, 1), lanes_b(kr, 1),
                           jnp.transpose(s0, (1, 2, 3, 0)))
    return y.transpose(3, 0, 1, 2).reshape(B * T, D_MODEL), jnp.transpose(st, (3, 0, 1, 2))


def _dot_split(x, m):
    hi = x.astype(BF16)
    lo = (x - hi.astype(F32)).astype(BF16)
    return jnp.dot(hi, m, preferred_element_type=F32) + jnp.dot(lo, m, preferred_element_type=F32)


def _head_matrices():
    lane_head = jnp.arange(D_MODEL)[:, None] // HEAD_DIM
    col = jnp.arange(LANES)[None, :]
    seg = jnp.stack([(col == lane_head + N_HEADS * j) for j in range(3)]).astype(BF16)
    return seg, jnp.swapaxes(seg, 1, 2)


def _rwkv_prep_kernel(x_ref, halo_ref, e_ref, g_ref, mu_ref, wrkv_ref, w1_ref, w2_ref, a1_ref, a2_ref, g1_ref,
                      g2_ref, vec_ref, seg_ref, spread_ref,
                      a_out, wr_out, w_out, b_out, k_out, v_out, gate_out, sc_out, *, seq_tiles, seq_new):
    i = pl.program_id(0)

    def norm(x):
        return x * lax.rsqrt(jnp.mean(x * x, axis=-1, keepdims=True) + NORM_EPS) * g_ref[...]

    h = norm(x_ref[...])
    row = lax.broadcasted_iota(jnp.int32, h.shape, 0)
    shifted = pltpu.roll(h, 1, axis=0)
    if seq_new:
        prev = jnp.where(row % seq_new == 0, e_ref[...], shifted)
    else:
        before = jnp.where(i % seq_tiles == 0, 0.0, norm(halo_ref[...])[SUBLANES - 1:])
        prev = jnp.where(row == 0, before, shifted)
    xx = prev - h
    mix = lambda n: (h + xx * mu_ref[n:n + 1]).astype(BF16)
    proj = lambda x, w: jnp.dot(x, w, preferred_element_type=F32)
    w0, a0, k_k, k_a, r_k = (vec_ref[n:n + 1] for n in range(5))
    r = proj(mix(0), wrkv_ref[0])
    k = proj(mix(1), wrkv_ref[1])
    v = proj(mix(2), wrkv_ref[2])
    z = -(w0 + proj(jnp.tanh(proj(mix(3), w1_ref[...])).astype(BF16), w2_ref[...]))
    softplus = jnp.maximum(z, 0.0) + jnp.log(1.0 + jnp.exp(-jnp.abs(z)))
    w = jnp.exp(-jnp.exp(-softplus - 0.5))
    a = jax.nn.sigmoid(a0 + proj(proj(mix(4), a1_ref[...]).astype(BF16), a2_ref[...]))
    gate_out[...] = proj(jax.nn.sigmoid(proj(mix(5), g1_ref[...])).astype(BF16), g2_ref[...])
    kk = k * k_k
    norm2 = _dot_split(_dot_split(kk * kk, seg_ref[0]), spread_ref[0])
    kk = kk / jnp.maximum(jnp.sqrt(norm2), 1e-12)
    k = k * (1.0 + (a - 1.0) * k_a)
    b = kk * a
    a_out[...] = -kk
    wr_out[...] = w * r
    w_out[...] = w
    b_out[...] = b
    k_out[...] = k
    v_out[...] = v
    sc_out[...] = (_dot_split(b * r, seg_ref[0]) + _dot_split(k * r, seg_ref[1])
                   + _dot_split(r * k * r_k, seg_ref[2]))


def rwkv_prep(x, e, g, mu, w_rkv, w1, w2, a1, a2, g1, g2, vec, seq_len, seq_new, tm=256):
    M, D = x.shape
    assert M % tm == 0 and (tm % seq_new == 0 if seq_new else seq_len % tm == 0)
    seg, spread = _head_matrices()
    once = pl.Buffered(1)
    row = pl.BlockSpec((tm, D), lambda i: (i, 0))
    halo = pl.BlockSpec((SUBLANES, D), lambda i: (jnp.maximum(i * (tm // SUBLANES) - 1, 0), 0))
    full = lambda t: pl.BlockSpec(t.shape, lambda i: (0,) * t.ndim, pipeline_mode=once)
    consts = [g.reshape(1, D), mu] + [t.astype(BF16) for t in (w_rkv, w1, w2, a1, a2, g1, g2)] + [vec, seg, spread]
    out = jax.ShapeDtypeStruct((M, D), F32)
    return pl.pallas_call(
        functools.partial(_rwkv_prep_kernel, seq_tiles=max(seq_len // tm, 1), seq_new=seq_new),
        grid=(M // tm,),
        in_specs=[row, halo, row] + [full(t) for t in consts],
        out_specs=[row] * 7 + [pl.BlockSpec((tm, LANES), lambda i: (i, 0))],
        out_shape=[out] * 7 + [jax.ShapeDtypeStruct((M, LANES), F32)],
        compiler_params=_params("parallel"),
        name="rwkv_prep",
    )(x, x, e, *consts)


def _rwkv_post_kernel(y_ref, v_ref, gate_ref, sc_ref, x_ref, wo_ref, lnw_ref, lnb_ref, seg_ref, spread_ref, o_ref):
    y = y_ref[...]
    inv = 1.0 / HEAD_DIM
    mean = _dot_split(_dot_split(y, seg_ref[0]), spread_ref[0]) * inv
    d = y - mean
    var = _dot_split(_dot_split(d * d, seg_ref[0]), spread_ref[0]) * inv
    yn = d * lax.rsqrt(var + GN_EPS) * lnw_ref[...] + lnb_ref[...]
    bonus = _dot_split(sc_ref[...], spread_ref[2]) * v_ref[...]
    out = ((yn + bonus) * gate_ref[...]).astype(BF16)
    o_ref[...] = x_ref[...] + jnp.dot(out, wo_ref[...], preferred_element_type=F32)


def rwkv_post(y, v, gate, sc, x, w_o, ln_w, ln_b, tm=512):
    M, D = x.shape
    assert M % tm == 0
    seg, spread = _head_matrices()
    once = pl.Buffered(1)
    row = pl.BlockSpec((tm, D), lambda i: (i, 0))
    full = lambda t: pl.BlockSpec(t.shape, lambda i: (0,) * t.ndim, pipeline_mode=once)
    consts = [w_o.astype(BF16), ln_w.reshape(1, D), ln_b.reshape(1, D), seg, spread]
    return pl.pallas_call(
        _rwkv_post_kernel,
        grid=(M // tm,),
        in_specs=[row, row, row, pl.BlockSpec((tm, LANES), lambda i: (i, 0)), row] + [full(t) for t in consts],
        out_specs=row,
        out_shape=jax.ShapeDtypeStruct((M, D), F32),
        compiler_params=_params("parallel"),
        name="rwkv_post",
    )(y, v, gate, sc, x, *consts)


def _band_attn_kernel(q_ref, k_ref, v_ref, cos_ref, sin_ref, o_ref, lse_ref, kr_ref, kprev, vprev):
    n = pl.program_id(1)

    @pl.when(n == 0)
    def _():
        kprev[...] = jnp.zeros_like(kprev)
        vprev[...] = jnp.zeros_like(vprev)

    i = lax.broadcasted_iota(jnp.int32, (BAND, 2 * BAND), 0)
    j = lax.broadcasted_iota(jnp.int32, (BAND, 2 * BAND), 1)
    valid = (j >= i) & (j <= i + BAND) & ((n > 0) | (j >= BAND))
    lane = lax.broadcasted_iota(jnp.int32, (BAND, LANES), 1)
    first_head = lane < HEAD_DIM
    first_half = lane % HEAD_DIM < HEAD_DIM // 2
    cos, sin = cos_ref[...], sin_ref[...]

    def rope(x):
        partner = jnp.where(first_half, pltpu.roll(x, LANES - HEAD_DIM // 2, axis=1),
                            pltpu.roll(x, HEAD_DIM // 2, axis=1))
        return x * cos + partner * sin

    k_before, v_before = kprev[...], vprev[...]
    key_head = lax.broadcasted_iota(jnp.int32, (2 * BAND, LANES), 1) < HEAD_DIM
    k_now, v_now = [], []
    for p in range(N_HEADS // 2):
        cs = slice(p * LANES, (p + 1) * LANES)
        q = rope(q_ref[0, :, cs])
        k = rope(k_ref[0, :, cs])
        kr_ref[0, :, cs] = k
        k_now.append(k.astype(BF16))
        v_now.append(v_ref[0, :, cs].astype(BF16))
        kcat = jnp.concatenate([k_before[:, cs], k_now[-1]], axis=0)
        vcat = jnp.concatenate([v_before[:, cs], v_now[-1]], axis=0)
        tops, sums = [], []
        for head_mask, own in ((first_head, key_head), (jnp.logical_not(first_head), jnp.logical_not(key_head))):
            qm = jnp.where(head_mask, q, 0.0).astype(BF16)
            s = lax.dot_general(qm, kcat, (((1,), (1,)), ((), ())), preferred_element_type=F32)
            s = jnp.where(valid, s * (HEAD_DIM ** -0.5), -jnp.inf)
            m = jnp.max(s, axis=-1, keepdims=True)
            e = jnp.exp(s - m).astype(BF16)
            sums.append(jnp.dot(e, jnp.where(own, vcat, 1.0), preferred_element_type=F32))
            tops.append(m)
        den = pltpu.roll(jnp.where(first_head, sums[1], sums[0]), HEAD_DIM, axis=1)
        o_ref[0, :, cs] = jnp.where(first_head, sums[0], sums[1]) / den
        lse_ref[0, :, cs] = jnp.where(first_head, tops[0], tops[1]) + jnp.log(den)
    kprev[...] = jnp.concatenate(k_now, axis=1)
    vprev[...] = jnp.concatenate(v_now, axis=1)


def band_attn(qkv, cos, sin, dil):
    S, L, W = qkv.shape
    D = W // 3
    nb = L // BAND
    col = lambda m: pl.BlockSpec((1, BAND, D), lambda s, n: (s, n, m))
    tab = pl.BlockSpec((BAND, LANES), lambda s, n: ((s % dil) * nb + n, 0))
    out = jax.ShapeDtypeStruct((S, L, D), F32)
    return pl.pallas_call(
        _band_attn_kernel,
        grid=(S, nb),
        in_specs=[col(0), col(1), col(2), tab, tab],
        out_specs=[col(0)] * 3,
        out_shape=[out] * 3,
        scratch_shapes=[pltpu.VMEM((BAND, D), BF16), pltpu.VMEM((BAND, D), BF16)],
        compiler_params=_params("parallel", "arbitrary"),
        name="band_attn",
    )(qkv, qkv, qkv, cos, sin)


def _norm_classes_kernel(x_ref, g_ref, *refs, dils):
    outs, scr = refs[:len(dils)], refs[len(dils)]
    x = x_ref[...]
    h = x * lax.rsqrt(jnp.mean(x * x, axis=-1, keepdims=True) + NORM_EPS) * g_ref[...]
    for j in range(scr.shape[0]):
        scr[j] = h[:, j * LANES:(j + 1) * LANES]
    for ref, dil in zip(outs, dils):
        if dil == 1:
            ref[0, 0] = h.astype(BF16)
            continue
        for c in range(dil):
            for j in range(scr.shape[0]):
                ref[0, c, :, j * LANES:(j + 1) * LANES] = scr[j, pl.ds(c, ref.shape[2], stride=dil), :].astype(BF16)


def norm_classes(x, g, dils, seq_len, tm=256):
    M, D = x.shape
    assert seq_len % tm == 0 and all(tm % (d * 2 * SUBLANES) == 0 for d in dils)
    per_seq = seq_len // tm
    spec = lambda d: pl.BlockSpec((1, d, tm // d, D), lambda i: (i // per_seq, 0, i % per_seq, 0))
    shape = lambda d: jax.ShapeDtypeStruct((M // seq_len, d, seq_len // d, D), BF16)
    outs = pl.pallas_call(
        functools.partial(_norm_classes_kernel, dils=tuple(dils)),
        grid=(M // tm,),
        in_specs=[pl.BlockSpec((tm, D), lambda i: (i, 0)), pl.BlockSpec((1, D), lambda i: (0, 0))],
        out_specs=[spec(d) for d in dils],
        out_shape=[shape(d) for d in dils],
        scratch_shapes=[pltpu.VMEM((D // LANES, tm, LANES), F32)],
        compiler_params=_params("parallel"),
        name="norm_classes",
    )(x, g.reshape(1, D))
    return [o.reshape(M, D) for o in outs]


def _attn_merge_kernel(*refs, dils):
    n = len(dils)
    o_refs, l_refs = refs[:n], refs[n:2 * n]
    x_ref, wo_ref, out_ref, scr = refs[2 * n:]

    def token_major(ref, dil):
        if dil == 1:
            return ref[0, 0]
        for c in range(dil):
            for j in range(scr.shape[0]):
                scr[j, pl.ds(c, ref.shape[2], stride=dil), :] = ref[0, c, :, j * LANES:(j + 1) * LANES]
        return jnp.concatenate([scr[j] for j in range(scr.shape[0])], axis=1)

    outs = [token_major(r, d) for r, d in zip(o_refs, dils)]
    lses = [token_major(r, d) for r, d in zip(l_refs, dils)]
    top = functools.reduce(jnp.maximum, lses)
    wts = [jnp.exp(l - top) for l in lses]
    merged = sum(w * o for w, o in zip(wts, outs)) / sum(wts)
    out_ref[...] = x_ref[...] + jnp.dot(merged.astype(BF16), wo_ref[...], preferred_element_type=F32)


def attn_merge(outs, lses, dils, x, w_o, seq_len, tm=256):
    M, D = x.shape
    assert seq_len % tm == 0 and all(tm % (d * SUBLANES) == 0 for d in dils)
    per_seq = seq_len // tm
    group = lambda t, d: t.reshape(M // seq_len, d, seq_len // d, D)
    spec = lambda d: pl.BlockSpec((1, d, tm // d, D), lambda i: (i // per_seq, 0, i % per_seq, 0))
    row = pl.BlockSpec((tm, D), lambda i: (i, 0))
    return pl.pallas_call(
        functools.partial(_attn_merge_kernel, dils=tuple(dils)),
        grid=(M // tm,),
        in_specs=[spec(d) for d in dils] * 2 + [row, pl.BlockSpec((D, D), lambda i: (0, 0))],
        out_specs=row,
        out_shape=jax.ShapeDtypeStruct((M, D), F32),
        scratch_shapes=[pltpu.VMEM((D // LANES, tm, LANES), F32)],
        compiler_params=_params("parallel"),
        name="attn_merge",
    )(*[group(t, d) for t, d in zip(outs, dils)], *[group(t, d) for t, d in zip(lses, dils)], x, w_o.astype(BF16))


def _sample_attn_kernel(q_ref, kn_ref, vn_ref, ck0_ref, cv0_ref, ck1_ref, cv1_ref, ck2_ref, cv2_ref,
                        o_ref, *, n_new):
    scale = HEAD_DIM ** -0.5
    caches = ((ck0_ref, cv0_ref), (ck1_ref, cv1_ref), (ck2_ref, cv2_ref))
    jcol = lax.broadcasted_iota(jnp.int32, (SUBLANES, 1), 0)
    masks = []
    for ck_ref, _ in caches:
        rows = ck_ref.shape[-1]
        dil = rows // BAND
        qi = lax.broadcasted_iota(jnp.int32, (SUBLANES, rows), 0)
        ri = lax.broadcasted_iota(jnp.int32, (SUBLANES, rows), 1)
        masks.append(ri >= qi if dil == 1 else ri % dil == qi)

    def head(h, carry):
        sc, sn = [], []
        for g, (ck_ref, _) in enumerate(caches):
            q = q_ref[0, g, h]
            s = jnp.dot(q.astype(BF16), ck_ref[0, 0, h].astype(BF16), preferred_element_type=F32) * scale
            sc.append(jnp.where(masks[g], s, -jnp.inf))
            kn = kn_ref[0, g, h]
            for i in range(n_new):
                si = jnp.sum(q * kn[i:i + 1, :], axis=-1, keepdims=True) * scale
                sn.append(jnp.where(jcol >= i if g == 0 else jcol == i, si, -jnp.inf))
        m = functools.reduce(jnp.maximum, [jnp.max(s, axis=-1, keepdims=True) for s in sc] + sn)
        den = jnp.zeros((SUBLANES, 1), F32)
        acc = jnp.zeros((SUBLANES, HEAD_DIM), F32)
        for g, (_, cv_ref) in enumerate(caches):
            e = jnp.exp(sc[g] - m)
            den = den + jnp.sum(e, axis=-1, keepdims=True)
            acc = acc + lax.dot_general(e.astype(BF16), cv_ref[0, 0, h].astype(BF16),
                                        (((1,), (1,)), ((), ())), preferred_element_type=F32)
            vn = vn_ref[0, g, h]
            for i in range(n_new):
                en = jnp.exp(sn[g * n_new + i] - m)
                den = den + en
                acc = acc + en * vn[i:i + 1, :]
        o_ref[0, h] = acc / den
        return carry

    def head_pair(hh, carry):
        head(2 * hh, carry)
        return head(2 * hh + 1, carry)

    lax.fori_loop(0, N_HEADS // 2, head_pair, 0)


def sample_attn(q, kn, vn, caches_k, caches_v, n_new):
    B = q.shape[0]
    new_spec = pl.BlockSpec((1, N_GROUPS, N_HEADS, SUBLANES, HEAD_DIM), lambda b: (b, 0, 0, 0, 0))
    specs = [new_spec] * 3
    args = [q, kn, vn]
    for g in range(N_GROUPS):
        rows = caches_k[g].shape[2]
        assert rows == BAND * DILATIONS[g] and n_new <= SUBLANES and (g == 0 or n_new <= DILATIONS[g])
        spec = pl.BlockSpec((1, 1, N_HEADS, HEAD_DIM, rows), lambda b: (0, b, 0, 0, 0))
        specs += [spec, spec]
        args += [jnp.transpose(caches_k[g], (0, 1, 3, 4, 2)), jnp.transpose(caches_v[g], (0, 1, 3, 4, 2))]
    return pl.pallas_call(
        functools.partial(_sample_attn_kernel, n_new=n_new),
        grid=(B,),
        in_specs=specs,
        out_specs=pl.BlockSpec((1, N_HEADS, SUBLANES, HEAD_DIM), lambda b: (b, 0, 0, 0)),
        out_shape=jax.ShapeDtypeStruct((B, N_HEADS, SUBLANES, HEAD_DIM), F32),
        compiler_params=_params("parallel"),
        name="sample_attn",
    )(*args)


def _ffn_kernel(xp_ref, xs_ref, halo_ref, g_ref, wg_ref, wv_ref, cw_ref, cb_ref, wd_ref, wdl_ref, e1_ref, e2_ref,
                gf_ref, op_ref, os_ref, hn_scr, hh_scr, acc_scr, act_scr, *, seq_tiles, sample_tile, seq_new,
                final_norm):
    i = pl.program_id(0)
    j = pl.program_id(1)
    is_sample = i == sample_tile

    def norm(x):
        return (x * lax.rsqrt(jnp.mean(x * x, axis=-1, keepdims=True) + NORM_EPS) * g_ref[...]).astype(BF16)

    def x_tile():
        return jnp.where(is_sample, xs_ref[...], xp_ref[...])

    @pl.when(j == 0)
    def _():
        hn_scr[...] = norm(x_tile())
        hh_scr[...] = norm(halo_ref[...]).astype(F32)
        acc_scr[...] = jnp.zeros_like(acc_scr)
        act_scr[...] = jnp.zeros_like(act_scr)

    acc_scr[...] += jnp.dot(act_scr[...], wdl_ref[...], preferred_element_type=F32)

    hn = hn_scr[...]
    gate = jnp.dot(hn, wg_ref[...], preferred_element_type=F32)
    val = jnp.dot(hn, wv_ref[...], preferred_element_type=F32)
    gh = jnp.dot(hh_scr[...].astype(BF16), wg_ref[...], preferred_element_type=F32)
    gh = jnp.where(i % seq_tiles == 0, 0.0, gh)
    row = lax.broadcasted_iota(jnp.int32, gate.shape, 0)
    roll1 = pltpu.roll(gate, 1, axis=0)
    roll2 = pltpu.roll(gate, 2, axis=0)
    pos = row % seq_new
    prev1 = jnp.where(is_sample,
                      jnp.where(pos == 0, e1_ref[...], roll1),
                      jnp.where(row == 0, gh[7:8], roll1))
    prev2 = jnp.where(is_sample,
                      jnp.where(pos < 2, e2_ref[...], roll2),
                      jnp.where(row == 0, gh[6:7], jnp.where(row == 1, gh[7:8], roll2)))
    conv = cb_ref[...] + prev2 * cw_ref[0:1] + prev1 * cw_ref[1:2] + gate * cw_ref[2:3]
    act = (conv * jax.nn.sigmoid(conv) * val).astype(BF16)
    act_scr[...] = act

    @pl.when(j == pl.num_programs(1) - 1)
    def _():
        y = x_tile() + acc_scr[...] + jnp.dot(act_scr[...], wd_ref[...], preferred_element_type=F32)
        if final_norm:
            y = y * lax.rsqrt(jnp.mean(y * y, axis=-1, keepdims=True) + NORM_EPS) * gf_ref[...]

        @pl.when(is_sample)
        def _():
            os_ref[...] = y

        @pl.when(jnp.logical_not(is_sample))
        def _():
            op_ref[...] = y


def conv_glu_ffn(xp, xs, g, w_up, conv_w, conv_b, w_down, e1, e2, g_final, seq_len, seq_new, tm=512, tf=256):
    Mp, D = xp.shape
    F = w_down.shape[0]
    assert Mp % tm == 0 and F % tf == 0 and seq_len % tm == 0 and tm % seq_new == 0
    assert xs.shape == (tm, D) and e1.shape == (tm, F) and e2.shape == (tm, F)
    nf = F // tf
    sample_tile = Mp // tm
    prow = pl.BlockSpec((tm, D), lambda i, j: (jnp.minimum(i, sample_tile - 1), 0))
    srow = pl.BlockSpec((tm, D), lambda i, j: (0, 0))
    halo = pl.BlockSpec((SUBLANES, D),
                        lambda i, j: (jnp.clip(i * (tm // SUBLANES) - 1, 0, Mp // SUBLANES - 1), 0))
    vec = pl.BlockSpec((1, D), lambda i, j: (0, 0))
    fill = pl.BlockSpec((tm, tf), lambda i, j: (0, jnp.where(i == sample_tile, j, 0)))
    w_up = w_up.astype(BF16)
    w_down = w_down.astype(BF16)
    return pl.pallas_call(
        functools.partial(_ffn_kernel, seq_tiles=seq_len // tm, sample_tile=sample_tile, seq_new=seq_new,
                          final_norm=g_final is not None),
        grid=(sample_tile + 1, nf),
        in_specs=[prow, srow, halo, vec,
                  pl.BlockSpec((D, tf), lambda i, j: (0, j)),
                  pl.BlockSpec((D, tf), lambda i, j: (0, j + nf)),
                  pl.BlockSpec((CONV_W, tf), lambda i, j: (0, j)),
                  pl.BlockSpec((1, tf), lambda i, j: (0, j)),
                  pl.BlockSpec((tf, D), lambda i, j: (j, 0)),
                  pl.BlockSpec((tf, D), lambda i, j: (jnp.maximum(j - 1, 0), 0)),
                  fill, fill, vec],
        out_specs=[prow, srow],
        out_shape=[jax.ShapeDtypeStruct((Mp, D), F32), jax.ShapeDtypeStruct((tm, D), F32)],
        scratch_shapes=[pltpu.VMEM((tm, D), BF16), pltpu.VMEM((SUBLANES, D), F32), pltpu.VMEM((tm, D), F32),
                        pltpu.VMEM((tm, tf), BF16)],
        compiler_params=_params("arbitrary", "arbitrary"),
        name="conv_glu_ffn",
    )(xp, xs, xp, g.reshape(1, D), w_up, w_up, conv_w, conv_b.reshape(1, F), w_down, w_down, e1, e2,
      (g if g_final is None else g_final).reshape(1, D))


def _rmsnorm(x, g):
    return x * lax.rsqrt(jnp.mean(x * x, axis=-1, keepdims=True) + NORM_EPS) * g


def _rope_angles(pos):
    half = HEAD_DIM // 2
    inv = jnp.exp(-math.log(ROPE_THETA) * jnp.arange(half, dtype=F32) * 2.0 / HEAD_DIM)
    return pos.astype(F32)[:, None] * inv[None, :]


def _rope_tables(pos):
    ang = _rope_angles(pos)
    cos, sin = jnp.cos(ang), jnp.sin(ang)
    return jnp.tile(cos, (1, LANES // cos.shape[1])), jnp.tile(jnp.concatenate([-sin, sin], axis=1), (1, 2))


def _rope(x, pos):
    half = HEAD_DIM // 2
    ang = _rope_angles(pos)
    cos = jnp.cos(ang)[:, None, None, :]
    sin = jnp.sin(ang)[:, None, None, :]
    x1, x2 = x[..., :half], x[..., half:]
    return jnp.concatenate([x1 * cos - x2 * sin, x2 * cos + x1 * sin], axis=-1)


def kernel(x_prompt, x_sample, state_rwkv_shift, state_rwkv_wkv, cache_k_w128, cache_v_w128, cache_k_w512, cache_v_w512, cache_k_w2048, cache_v_w2048, state_ffn_conv, norm_mix, norm_ffn, norm_final, rwkv_mu, rwkv_w_rkv, rwkv_w0, rwkv_w1, rwkv_w2, rwkv_a0, rwkv_a1, rwkv_a2, rwkv_g1, rwkv_g2, rwkv_k_k, rwkv_k_a, rwkv_r_k, rwkv_ln_w, rwkv_ln_b, rwkv_w_o, attn_w_in, attn_w_o, ffn_w_up, ffn_conv_w, ffn_conv_b, ffn_w_down):
    Bp, Tp, D = x_prompt.shape
    Bs, Ts, _ = x_sample.shape
    Mp, Ms = Bp * Tp, Bs * Ts
    assert Ts >= CONV_W - 1 and Tp >= CONV_W - 1 and CONV_W == 3
    xp = x_prompt.reshape(Mp, D)
    xs = x_sample.reshape(Ms, D)
    caches_k = (cache_k_w128, cache_k_w512, cache_k_w2048)
    caches_v = (cache_v_w128, cache_v_w512, cache_v_w2048)

    p_shift, p_wkv, s_shift, s_wkv = [], [], [], []
    p_k = [[] for _ in range(N_GROUPS)]
    p_v = [[] for _ in range(N_GROUPS)]
    s_k = [[] for _ in range(N_GROUPS)]
    s_v = [[] for _ in range(N_GROUPS)]
    p_conv, s_conv = [], []

    def time_mix(i, li, x, carried, s0, B, T):
        vec = jnp.stack([rwkv_w0[li], rwkv_a0[li], rwkv_k_k[li], rwkv_k_a[li], rwkv_r_k[li].reshape(D)])
        a, wr, w, b, k, v, gate, sc = rwkv_prep(
            x, x if carried is None else carried, norm_mix[i], rwkv_mu[li], rwkv_w_rkv[li], rwkv_w1[li], rwkv_w2[li],
            rwkv_a1[li], rwkv_a2[li], rwkv_g1[li], rwkv_g2[li], vec, T, 0 if carried is None else T)
        if carried is None:
            y, st = wkv_scan_tokens([t.reshape(B, T, D) for t in (a, wr, w, b, k, v)], sc.reshape(B, T, LANES))
            y, st = y.reshape(B * T, D), _state_out(st, B)
        else:
            y, st = _wkv_carried(a, wr, w, b, k, v, sc[:, :N_HEADS], sc[:, N_HEADS:2 * N_HEADS], s0, B, T)
        return rwkv_post(y, v, gate, sc, x, rwkv_w_o[li], rwkv_ln_w[li], rwkv_ln_b[li]), st

    depth = norm_mix.shape[0]
    for i in range(depth):
        li = i // 2
        if i % 2 == 0:
            p_shift.append(_rmsnorm(xp.reshape(Bp, Tp, D)[:, -1], norm_mix[i]))
            s_shift.append(_rmsnorm(xs.reshape(Bs, Ts, D)[:, -1], norm_mix[i]))
            xp, stp = time_mix(i, li, xp, None, None, Bp, Tp)
            xs, sts = time_mix(i, li, xs, jnp.repeat(state_rwkv_shift[li], Ts, axis=0), state_rwkv_wkv[li], Bs, Ts)
            p_wkv.append(stp)
            s_wkv.append(sts)
        else:
            hs = _rmsnorm(xs, norm_mix[i])
            assert len(caches_k[0].shape) == 5 and caches_k[0].shape[0] == 1 and li == 0
            hgs = norm_classes(xp, norm_mix[i], DILATIONS, Tp)
            outs, lses = [], []
            for g in range(N_GROUPS):
                dil = DILATIONS[g]
                L = Tp // dil
                keep = min(WINDOWS[g], Tp)
                assert Tp % (dil * BAND) == 0 and keep % dil == 0
                qkv_g = mm(hgs[g], attn_w_in[li][:, g * 3 * D:(g + 1) * 3 * D]).reshape(Bp * dil, L, 3 * D)
                pos = (jnp.arange(dil)[:, None] + dil * jnp.arange(L)[None, :]).reshape(-1)
                o, lse, kr = band_attn(qkv_g, *_rope_tables(pos), dil)
                outs.append(o)
                lses.append(lse)
                tail = lambda t: t.reshape(Bp, dil, L, D)[:, :, L - keep // dil:].transpose(0, 2, 1, 3).reshape(
                    Bp, keep, N_HEADS, HEAD_DIM)
                p_k[g].append(tail(kr))
                p_v[g].append(tail(qkv_g[..., 2 * D:]))
            qkv_s = mm(hs.astype(BF16), attn_w_in[li]).reshape(Bs, Ts, N_GROUPS, 3, N_HEADS, HEAD_DIM)
            pos_s = PAST_LEN + jnp.arange(Ts)
            q_s = _rope(qkv_s[:, :, :, 0], pos_s)
            k_s = _rope(qkv_s[:, :, :, 1], pos_s)
            v_s = qkv_s[:, :, :, 2]
            rows8 = lambda t: jnp.pad(t.transpose(0, 2, 3, 1, 4), ((0, 0),) * 3 + ((0, SUBLANES - Ts), (0, 0)))
            osmp = sample_attn(rows8(q_s), rows8(k_s), rows8(v_s), caches_k, caches_v, Ts)
            osmp = osmp[:, :, :Ts].transpose(0, 2, 1, 3).reshape(Ms, D)
            for g in range(N_GROUPS):
                s_k[g].append(k_s[:, :, g])
                s_v[g].append(v_s[:, :, g])
            xp = attn_merge(outs, lses, DILATIONS, xp, attn_w_o[li], Tp)
            xs = xs + mm(osmp, attn_w_o[li])

        buf = state_ffn_conv[i]
        first = (jnp.arange(Ms) % Ts == 0)[:, None]
        e1 = jnp.repeat(buf[:, 1], Ts, axis=0)
        e2 = jnp.where(first, jnp.repeat(buf[:, 0], Ts, axis=0), e1)
        tail = jnp.concatenate([xp.reshape(Bp, Tp, D)[:, Tp - 2:].reshape(Bp * 2, D),
                                xs.reshape(Bs, Ts, D)[:, Ts - 2:].reshape(Bs * 2, D)], axis=0)
        gate_tail = mm(_rmsnorm(tail, norm_ffn[i]), ffn_w_up[i][:, :D_FF])
        p_conv.append(gate_tail[:Bp * 2].reshape(Bp, 2, D_FF))
        s_conv.append(gate_tail[Bp * 2:].reshape(Bs, 2, D_FF))
        xp, xs = conv_glu_ffn(xp, xs, norm_ffn[i], ffn_w_up[i], ffn_conv_w[i], ffn_conv_b[i], ffn_w_down[i],
                              e1, e2, norm_final if i == depth - 1 else None, Tp, Ts)

    y_prompt = xp.reshape(Bp, Tp, D)
    y_sample = xs.reshape(Bs, Ts, D)
    st = jnp.stack
    return (y_prompt, y_sample,
            st(p_shift), st(p_wkv),
            st(p_k[0]), st(p_v[0]), st(p_k[1]), st(p_v[1]), st(p_k[2]), st(p_v[2]), st(p_conv),
            st(s_shift), st(s_wkv),
            st(s_k[0]), st(s_v[0]), st(s_k[1]), st(s_v[1]), st(s_k[2]), st(s_v[2]), st(s_conv))
```

```python
import functools
import math

import jax
import jax.numpy as jnp
from jax import lax
from jax.experimental import pallas as pl
from jax.experimental.pallas import tpu as pltpu

F32 = jnp.float32
BF16 = jnp.bfloat16

D_MODEL = 1024
HEAD_DIM = 64
N_HEADS = 16
D_FF = 2816
CONV_W = 3
N_GROUPS = 3
WINDOWS = (128, 512, 2048)
DILATIONS = (1, 4, 16)
BAND = 128
GN_EPS = 64e-5
NORM_EPS = 1e-6
ROPE_THETA = 10000.0
PAST_LEN = 2048
LANES = 128
SUBLANES = 8
VMEM_LIMIT = 56 * 1024 * 1024


def _params(*sem):
    return pltpu.CompilerParams(dimension_semantics=sem, vmem_limit_bytes=VMEM_LIMIT)


def _mm_kernel(x_ref, w_ref, o_ref):
    o_ref[...] = jnp.dot(x_ref[...].astype(BF16), w_ref[...],
                         preferred_element_type=F32).astype(o_ref.dtype)


def mm(x, w, tm=1024, tn=1024):
    M, K = x.shape
    N = w.shape[1]
    tm = next(t for t in (tm, tm // 2, M) if M % t == 0)
    tn = next(t for t in (tn, tn // 2, N) if N % t == 0)
    return pl.pallas_call(
        _mm_kernel,
        grid=(M // tm, N // tn),
        in_specs=[pl.BlockSpec((tm, K), lambda i, j: (i, 0)),
                  pl.BlockSpec((K, tn), lambda i, j: (0, j))],
        out_specs=pl.BlockSpec((tm, tn), lambda i, j: (i, j)),
        out_shape=jax.ShapeDtypeStruct((M, N), F32),
        compiler_params=_params("parallel", "parallel"),
        name="mm",
    )(x, w.astype(BF16))


VALUE_ROWS = 4


def _wkv_state_kernel(a_ref, wr_ref, w_ref, b_ref, k_ref, v_ref, br_ref, kr_ref, s_ref, y_ref, so_ref, *, steps):
    def rows(c, carry):
        v0 = pl.multiple_of(c * VALUE_ROWS, VALUE_ROWS)
        s = s_ref[0, pl.ds(v0, VALUE_ROWS)]
        for t in range(steps):
            sa = jnp.sum(s * a_ref[t, 0][None], axis=1, keepdims=True)
            yy = jnp.sum(s * wr_ref[t, 0][None], axis=1, keepdims=True)
            v = v_ref[t, 0, pl.ds(v0, VALUE_ROWS), :][:, None, :]
            y = yy + sa * br_ref[t, 0][None] + v * kr_ref[t, 0][None]
            y_ref[t, 0, pl.ds(v0, VALUE_ROWS), :] = y[:, 0, :]
            s = s * w_ref[t, 0][None] + sa * b_ref[t, 0][None] + v * k_ref[t, 0][None]
        so_ref[0, pl.ds(v0, VALUE_ROWS)] = s
        return carry

    lax.fori_loop(0, HEAD_DIM // VALUE_ROWS, rows, 0)


def wkv_scan_state(ops, br, kr, s0):
    T, H, dh, B = ops[0].shape
    op_spec = pl.BlockSpec((T, 1, dh, B), lambda h: (0, h, 0, 0))
    sc_spec = pl.BlockSpec((T, 1, 1, B), lambda h: (0, h, 0, 0))
    st_spec = pl.BlockSpec((1, dh, dh, B), lambda h: (h, 0, 0, 0))
    return pl.pallas_call(
        functools.partial(_wkv_state_kernel, steps=T),
        grid=(H,),
        in_specs=[op_spec] * 6 + [sc_spec, sc_spec, st_spec],
        out_specs=[op_spec, st_spec],
        out_shape=[jax.ShapeDtypeStruct((T, H, dh, B), F32), jax.ShapeDtypeStruct(s0.shape, F32)],
        compiler_params=_params("parallel"),
        name="wkv_scan_state",
    )(*ops, br, kr, s0)


CHUNK = LANES
KEYS_PER_TRIP = 32


def _wkv_chunk_kernel(a_ref, wr_ref, w_ref, b_ref, k_ref, v_ref, sc_ref, y_ref, st_ref,
                      state, zrow, zv, zs, zy, xt):
    c = pl.program_id(0)
    nb = a_ref.shape[0]
    rows_bh = nb * N_HEADS

    @pl.when(c == 0)
    def _():
        state[...] = jnp.zeros_like(state)

    def lanes_tile(ref, first, second, stride):
        top = ref[:, pl.ds(first, N_HEADS, stride=stride), :].reshape(rows_bh, CHUNK)
        bot = ref[:, pl.ds(second, N_HEADS, stride=stride), :].reshape(rows_bh, CHUNK)
        return jnp.concatenate([top, bot], axis=0).T

    def grouped(n, body):
        def trip(g, carry):
            for u in range(SUBLANES):
                body(g * SUBLANES + u)
            return carry
        lax.fori_loop(0, n // SUBLANES, trip, 0)

    for op, ref in enumerate((a_ref, wr_ref, w_ref, b_ref, k_ref)):
        def fill(k, op=op, ref=ref):
            zrow[op, k] = lanes_tile(ref, k, k, HEAD_DIM)
        grouped(HEAD_DIM, fill)

    def fill_v(v_lo):
        zv[pl.ds(pl.multiple_of(v_lo * CHUNK, CHUNK), CHUNK), :] = lanes_tile(
            v_ref, v_lo, HEAD_DIM // 2 + v_lo, HEAD_DIM)
    grouped(HEAD_DIM // 2, fill_v)

    zs[0] = lanes_tile(sc_ref, 0, 0, 1)
    zs[1] = lanes_tile(sc_ref, N_HEADS, N_HEADS, 1)

    def row(op, k, t):
        return jnp.broadcast_to(zrow[op, k, pl.ds(t, 1), :], (SUBLANES, LANES))[None]

    def step(t, carry):
        def reduce_keys(g, acc):
            sa, yy = acc
            for u in range(KEYS_PER_TRIP):
                k = g * KEYS_PER_TRIP + u
                sk = state[k]
                sa = sa + sk * row(0, k, t)
                yy = yy + sk * row(1, k, t)
            return sa, yy
        zero = jnp.zeros((4, SUBLANES, LANES), F32)
        sa, yy = lax.fori_loop(0, HEAD_DIM // KEYS_PER_TRIP, reduce_keys, (zero, zero))
        v = zv[pl.ds(t, HEAD_DIM // 2, stride=CHUNK), :].reshape(4, SUBLANES, LANES)
        br = jnp.broadcast_to(zs[0, pl.ds(t, 1), :], (SUBLANES, LANES))[None]
        kr = jnp.broadcast_to(zs[1, pl.ds(t, 1), :], (SUBLANES, LANES))[None]
        zy[pl.ds(t, HEAD_DIM // 2, stride=CHUNK), :] = (yy + sa * br + v * kr).reshape(HEAD_DIM // 2, LANES)

        def update_keys(g, carry):
            for u in range(KEYS_PER_TRIP):
                k = g * KEYS_PER_TRIP + u
                state[k] = state[k] * row(2, k, t) + sa * row(3, k, t) + v * row(4, k, t)
            return carry
        lax.fori_loop(0, HEAD_DIM // KEYS_PER_TRIP, update_keys, 0)
        return carry

    lax.fori_loop(0, CHUNK, step, 0)

    def drain(v_lo):
        tile = zy[pl.ds(pl.multiple_of(v_lo * CHUNK, CHUNK), CHUNK), :].T
        for v_hi in range(2):
            xt[:, pl.ds(v_hi * (HEAD_DIM // 2) + v_lo, N_HEADS, stride=HEAD_DIM), :] = (
                tile[v_hi * rows_bh:(v_hi + 1) * rows_bh].reshape(nb, N_HEADS, CHUNK))
    grouped(HEAD_DIM // 2, drain)

    def to_token_major(b, carry):
        for cb in range(y_ref.shape[2] // LANES):
            y_ref[b, :, cb * LANES:(cb + 1) * LANES] = xt[b, cb * LANES:(cb + 1) * LANES, :].T
        return carry
    lax.fori_loop(0, nb, to_token_major, 0)

    @pl.when(c == pl.num_programs(0) - 1)
    def _():
        st_ref[...] = state[...]


def wkv_scan_tokens(ops, sc):
    B, D, T = ops[0].shape
    assert B * N_HEADS * 2 == LANES and T % CHUNK == 0 and D == N_HEADS * HEAD_DIM
    once = pl.Buffered(1)
    spec = pl.BlockSpec((B, D, CHUNK), lambda c: (0, 0, c), pipeline_mode=once)
    st_shape = (HEAD_DIM, 4, SUBLANES, LANES)
    return pl.pallas_call(
        _wkv_chunk_kernel,
        grid=(T // CHUNK,),
        in_specs=[spec] * 6 + [pl.BlockSpec((B, LANES, CHUNK), lambda c: (0, 0, c), pipeline_mode=once)],
        out_specs=[pl.BlockSpec((B, CHUNK, D), lambda c: (0, c, 0)), pl.BlockSpec(st_shape, lambda c: (0, 0, 0, 0))],
        out_shape=[jax.ShapeDtypeStruct((B, T, D), F32), jax.ShapeDtypeStruct(st_shape, F32)],
        scratch_shapes=[pltpu.VMEM(st_shape, F32),
                        pltpu.VMEM((5, HEAD_DIM, CHUNK, LANES), F32),
                        pltpu.VMEM((HEAD_DIM // 2 * CHUNK, LANES), F32),
                        pltpu.VMEM((2, CHUNK, LANES), F32),
                        pltpu.VMEM((HEAD_DIM // 2 * CHUNK, LANES), F32),
                        pltpu.VMEM((B, D, CHUNK), F32)],
        compiler_params=_params("arbitrary"),
        name="wkv_scan_tokens",
    )(*ops, sc)


def _state_out(s, B):
    nl = B * N_HEADS // 64
    s = s.reshape(HEAD_DIM, 32, nl, 2, 64).transpose(2, 4, 3, 1, 0)
    return s.reshape(B, N_HEADS, HEAD_DIM, HEAD_DIM)


def _wkv_carried(a, wr, w, b, k, v, br, kr, s0, B, T):
    lanes_b = lambda t, n: t.reshape(B, T, N_HEADS, n).transpose(1, 2, 3, 0)
    y, st = wkv_scan_state([lanes_b(t, HEAD_DIM) for t in (a, wr, w, b, k, v)], lanes_b(br, 1), lanes_b(kr, 1),
                           jnp.transpose(s0, (1, 2, 3, 0)))
    return y.transpose(3, 0, 1, 2).reshape(B * T, D_MODEL), jnp.transpose(st, (3, 0, 1, 2))


def _dot_split(x, m):
    hi = x.astype(BF16)
    lo = (x - hi.astype(F32)).astype(BF16)
    return jnp.dot(hi, m, preferred_element_type=F32) + jnp.dot(lo, m, preferred_element_type=F32)


def _head_matrices():
    lane_head = jnp.arange(D_MODEL)[:, None] // HEAD_DIM
    col = jnp.arange(LANES)[None, :]
    seg = jnp.stack([(col == lane_head + N_HEADS * j) for j in range(3)]).astype(BF16)
    return seg, jnp.swapaxes(seg, 1, 2)


def _rwkv_prep_kernel(x_ref, halo_ref, e_ref, g_ref, mu_ref, wrkv_ref, w1_ref, w2_ref, a1_ref, a2_ref, g1_ref,
                      g2_ref, vec_ref, seg_ref, spread_ref, *outs, seq_tiles, seq_new):
    i = pl.program_id(0)

    def norm(x):
        return x * lax.rsqrt(jnp.mean(x * x, axis=-1, keepdims=True) + NORM_EPS) * g_ref[...]

    h = norm(x_ref[...])
    row = lax.broadcasted_iota(jnp.int32, h.shape, 0)
    shifted = pltpu.roll(h, 1, axis=0)
    if seq_new:
        prev = jnp.where(row % seq_new == 0, e_ref[...], shifted)
    else:
        before = jnp.where(i % seq_tiles == 0, 0.0, norm(halo_ref[...])[SUBLANES - 1:])
        prev = jnp.where(row == 0, before, shifted)
    xx = prev - h
    mix = lambda n: (h + xx * mu_ref[n:n + 1]).astype(BF16)
    proj = lambda x, w: jnp.dot(x, w, preferred_element_type=F32)
    w0, a0, k_k, k_a, r_k = (vec_ref[n:n + 1] for n in range(5))
    r = proj(mix(0), wrkv_ref[0])
    k = proj(mix(1), wrkv_ref[1])
    v = proj(mix(2), wrkv_ref[2])
    z = -(w0 + proj(jnp.tanh(proj(mix(3), w1_ref[...])).astype(BF16), w2_ref[...]))
    softplus = jnp.maximum(z, 0.0) + jnp.log(1.0 + jnp.exp(-jnp.abs(z)))
    w = jnp.exp(-jnp.exp(-softplus - 0.5))
    a = jax.nn.sigmoid(a0 + proj(proj(mix(4), a1_ref[...]).astype(BF16), a2_ref[...]))
    gate = proj(jax.nn.sigmoid(proj(mix(5), g1_ref[...])).astype(BF16), g2_ref[...])
    kk = k * k_k
    norm2 = _dot_split(_dot_split(kk * kk, seg_ref[0]), spread_ref[0])
    kk = kk / jnp.maximum(jnp.sqrt(norm2), 1e-12)
    k = k * (1.0 + (a - 1.0) * k_a)
    b = kk * a
    sc = _dot_split(b * r, seg_ref[0]) + _dot_split(k * r, seg_ref[1]) + _dot_split(r * k * r_k, seg_ref[2])
    scan_ops = (-kk, w * r, w, b, k, v)
    if seq_new:
        for ref, val in zip(outs[:6], scan_ops):
            ref[...] = val
        gate_out, sc_out = outs[6:]
    else:
        for ref, val in zip(outs[:6], scan_ops):
            ref[0] = val.T
        outs[6][0] = sc.T
        v_out, gate_out, sc_out = outs[7:]
        v_out[...] = v
    gate_out[...] = gate
    sc_out[...] = sc


def rwkv_prep(x, e, g, mu, w_rkv, w1, w2, a1, a2, g1, g2, vec, seq_len, seq_new, tm=256):
    M, D = x.shape
    assert M % tm == 0 and (tm % seq_new == 0 if seq_new else seq_len % tm == 0)
    seg, spread = _head_matrices()
    once = pl.Buffered(1)
    row = pl.BlockSpec((tm, D), lambda i: (i, 0))
    halo = pl.BlockSpec((SUBLANES, D), lambda i: (jnp.maximum(i * (tm // SUBLANES) - 1, 0), 0))
    full = lambda t: pl.BlockSpec(t.shape, lambda i: (0,) * t.ndim, pipeline_mode=once)
    consts = [g.reshape(1, D), mu] + [t.astype(BF16) for t in (w_rkv, w1, w2, a1, a2, g1, g2)] + [vec, seg, spread]
    out = jax.ShapeDtypeStruct((M, D), F32)
    packed = pl.BlockSpec((tm, LANES), lambda i: (i, 0))
    packed_out = jax.ShapeDtypeStruct((M, LANES), F32)
    if seq_new:
        out_specs = [row] * 7 + [packed]
        out_shape = [out] * 7 + [packed_out]
    else:
        per_seq = seq_len // tm
        cm = lambda n: pl.BlockSpec((1, n, tm), lambda i: (i // per_seq, 0, i % per_seq))
        cm_out = lambda n: jax.ShapeDtypeStruct((M // seq_len, n, seq_len), F32)
        out_specs = [cm(D)] * 6 + [cm(LANES), row, row, packed]
        out_shape = [cm_out(D)] * 6 + [cm_out(LANES), out, out, packed_out]
    return pl.pallas_call(
        functools.partial(_rwkv_prep_kernel, seq_tiles=max(seq_len // tm, 1), seq_new=seq_new),
        grid=(M // tm,),
        in_specs=[row, halo, row if seq_new else pl.BlockSpec((tm, D), lambda i: (0, 0))] + [full(t) for t in consts],
        out_specs=out_specs,
        out_shape=out_shape,
        compiler_params=_params("parallel"),
        name="rwkv_prep",
    )(x, x, e, *consts)


def _rwkv_post_kernel(y_ref, v_ref, gate_ref, sc_ref, x_ref, wo_ref, lnw_ref, lnb_ref, seg_ref, spread_ref, o_ref):
    y = y_ref[...]
    inv = 1.0 / HEAD_DIM
    mean = _dot_split(_dot_split(y, seg_ref[0]), spread_ref[0]) * inv
    d = y - mean
    var = _dot_split(_dot_split(d * d, seg_ref[0]), spread_ref[0]) * inv
    yn = d * lax.rsqrt(var + GN_EPS) * lnw_ref[...] + lnb_ref[...]
    bonus = _dot_split(sc_ref[...], spread_ref[2]) * v_ref[...]
    out = ((yn + bonus) * gate_ref[...]).astype(BF16)
    o_ref[...] = x_ref[...] + jnp.dot(out, wo_ref[...], preferred_element_type=F32)


def rwkv_post(y, v, gate, sc, x, w_o, ln_w, ln_b, tm=512):
    M, D = x.shape
    assert M % tm == 0
    seg, spread = _head_matrices()
    once = pl.Buffered(1)
    row = pl.BlockSpec((tm, D), lambda i: (i, 0))
    full = lambda t: pl.BlockSpec(t.shape, lambda i: (0,) * t.ndim, pipeline_mode=once)
    consts = [w_o.astype(BF16), ln_w.reshape(1, D), ln_b.reshape(1, D), seg, spread]
    return pl.pallas_call(
        _rwkv_post_kernel,
        grid=(M // tm,),
        in_specs=[row, row, row, pl.BlockSpec((tm, LANES), lambda i: (i, 0)), row] + [full(t) for t in consts],
        out_specs=row,
        out_shape=jax.ShapeDtypeStruct((M, D), F32),
        compiler_params=_params("parallel"),
        name="rwkv_post",
    )(y, v, gate, sc, x, *consts)


def _band_attn_kernel(q_ref, k_ref, v_ref, cos_ref, sin_ref, o_ref, lse_ref, kr_ref, kprev, vprev):
    n = pl.program_id(1)

    @pl.when(n == 0)
    def _():
        kprev[...] = jnp.zeros_like(kprev)
        vprev[...] = jnp.zeros_like(vprev)

    i = lax.broadcasted_iota(jnp.int32, (BAND, 2 * BAND), 0)
    j = lax.broadcasted_iota(jnp.int32, (BAND, 2 * BAND), 1)
    valid = (j >= i) & (j <= i + BAND) & ((n > 0) | (j >= BAND))
    lane = lax.broadcasted_iota(jnp.int32, (BAND, LANES), 1)
    first_head = lane < HEAD_DIM
    first_half = lane % HEAD_DIM < HEAD_DIM // 2
    cos, sin = cos_ref[...], sin_ref[...]

    def rope(x):
        partner = jnp.where(first_half, pltpu.roll(x, LANES - HEAD_DIM // 2, axis=1),
                            pltpu.roll(x, HEAD_DIM // 2, axis=1))
        return x * cos + partner * sin

    k_before, v_before = kprev[...], vprev[...]
    key_head = lax.broadcasted_iota(jnp.int32, (2 * BAND, LANES), 1) < HEAD_DIM
    k_now, v_now = [], []
    for p in range(N_HEADS // 2):
        cs = slice(p * LANES, (p + 1) * LANES)
        q = rope(q_ref[0, :, cs])
        k = rope(k_ref[0, :, cs])
        kr_ref[0, :, cs] = k
        k_now.append(k.astype(BF16))
        v_now.append(v_ref[0, :, cs].astype(BF16))
        kcat = jnp.concatenate([k_before[:, cs], k_now[-1]], axis=0)
        vcat = jnp.concatenate([v_before[:, cs], v_now[-1]], axis=0)
        tops, sums = [], []
        for head_mask, own in ((first_head, key_head), (jnp.logical_not(first_head), jnp.logical_not(key_head))):
            qm = jnp.where(head_mask, q, 0.0).astype(BF16)
            s = lax.dot_general(qm, kcat, (((1,), (1,)), ((), ())), preferred_element_type=F32)
            s = jnp.where(valid, s * (HEAD_DIM ** -0.5), -jnp.inf)
            m = jnp.max(s, axis=-1, keepdims=True)
            e = jnp.exp(s - m).astype(BF16)
            sums.append(jnp.dot(e, jnp.where(own, vcat, 1.0), preferred_element_type=F32))
            tops.append(m)
        den = pltpu.roll(jnp.where(first_head, sums[1], sums[0]), HEAD_DIM, axis=1)
        o_ref[0, :, cs] = jnp.where(first_head, sums[0], sums[1]) / den
        lse_ref[0, :, cs] = jnp.where(first_head, tops[0], tops[1]) + jnp.log(den)
    kprev[...] = jnp.concatenate(k_now, axis=1)
    vprev[...] = jnp.concatenate(v_now, axis=1)


def band_attn(qkv, cos, sin, dil):
    S, L, W = qkv.shape
    D = W // 3
    nb = L // BAND
    col = lambda m: pl.BlockSpec((1, BAND, D), lambda s, n: (s, n, m))
    tab = pl.BlockSpec((BAND, LANES), lambda s, n: ((s % dil) * nb + n, 0))
    out = jax.ShapeDtypeStruct((S, L, D), F32)
    return pl.pallas_call(
        _band_attn_kernel,
        grid=(S, nb),
        in_specs=[col(0), col(1), col(2), tab, tab],
        out_specs=[col(0)] * 3,
        out_shape=[out] * 3,
        scratch_shapes=[pltpu.VMEM((BAND, D), BF16), pltpu.VMEM((BAND, D), BF16)],
        compiler_params=_params("parallel", "arbitrary"),
        name="band_attn",
    )(qkv, qkv, qkv, cos, sin)


def _norm_classes_kernel(x_ref, g_ref, *refs, dils):
    outs, scr = refs[:len(dils)], refs[len(dils)]
    x = x_ref[...]
    h = x * lax.rsqrt(jnp.mean(x * x, axis=-1, keepdims=True) + NORM_EPS) * g_ref[...]
    for j in range(scr.shape[0]):
        scr[j] = h[:, j * LANES:(j + 1) * LANES]
    for ref, dil in zip(outs, dils):
        if dil == 1:
            ref[0, 0] = h.astype(BF16)
            continue
        for c in range(dil):
            for j in range(scr.shape[0]):
                ref[0, c, :, j * LANES:(j + 1) * LANES] = scr[j, pl.ds(c, ref.shape[2], stride=dil), :].astype(BF16)


def norm_classes(x, g, dils, seq_len, tm=256):
    M, D = x.shape
    assert seq_len % tm == 0 and all(tm % (d * 2 * SUBLANES) == 0 for d in dils)
    per_seq = seq_len // tm
    spec = lambda d: pl.BlockSpec((1, d, tm // d, D), lambda i: (i // per_seq, 0, i % per_seq, 0))
    shape = lambda d: jax.ShapeDtypeStruct((M // seq_len, d, seq_len // d, D), BF16)
    outs = pl.pallas_call(
        functools.partial(_norm_classes_kernel, dils=tuple(dils)),
        grid=(M // tm,),
        in_specs=[pl.BlockSpec((tm, D), lambda i: (i, 0)), pl.BlockSpec((1, D), lambda i: (0, 0))],
        out_specs=[spec(d) for d in dils],
        out_shape=[shape(d) for d in dils],
        scratch_shapes=[pltpu.VMEM((D // LANES, tm, LANES), F32)],
        compiler_params=_params("parallel"),
        name="norm_classes",
    )(x, g.reshape(1, D))
    return [o.reshape(M, D) for o in outs]


def _attn_merge_kernel(*refs, dils):
    n = len(dils)
    o_refs, l_refs = refs[:n], refs[n:2 * n]
    x_ref, wo_ref, out_ref, scr = refs[2 * n:]

    def token_major(ref, dil):
        if dil == 1:
            return ref[0, 0]
        for c in range(dil):
            for j in range(scr.shape[0]):
                scr[j, pl.ds(c, ref.shape[2], stride=dil), :] = ref[0, c, :, j * LANES:(j + 1) * LANES]
        return jnp.concatenate([scr[j] for j in range(scr.shape[0])], axis=1)

    outs = [token_major(r, d) for r, d in zip(o_refs, dils)]
    lses = [token_major(r, d) for r, d in zip(l_refs, dils)]
    top = functools.reduce(jnp.maximum, lses)
    wts = [jnp.exp(l - top) for l in lses]
    merged = sum(w * o for w, o in zip(wts, outs)) / sum(wts)
    out_ref[...] = x_ref[...] + jnp.dot(merged.astype(BF16), wo_ref[...], preferred_element_type=F32)


def attn_merge(outs, lses, dils, x, w_o, seq_len, tm=256):
    M, D = x.shape
    assert seq_len % tm == 0 and all(tm % (d * SUBLANES) == 0 for d in dils)
    per_seq = seq_len // tm
    group = lambda t, d: t.reshape(M // seq_len, d, seq_len // d, D)
    spec = lambda d: pl.BlockSpec((1, d, tm // d, D), lambda i: (i // per_seq, 0, i % per_seq, 0))
    row = pl.BlockSpec((tm, D), lambda i: (i, 0))
    return pl.pallas_call(
        functools.partial(_attn_merge_kernel, dils=tuple(dils)),
        grid=(M // tm,),
        in_specs=[spec(d) for d in dils] * 2 + [row, pl.BlockSpec((D, D), lambda i: (0, 0))],
        out_specs=row,
        out_shape=jax.ShapeDtypeStruct((M, D), F32),
        scratch_shapes=[pltpu.VMEM((D // LANES, tm, LANES), F32)],
        compiler_params=_params("parallel"),
        name="attn_merge",
    )(*[group(t, d) for t, d in zip(outs, dils)], *[group(t, d) for t, d in zip(lses, dils)], x, w_o.astype(BF16))


def _sample_attn_kernel(q_ref, kn_ref, vn_ref, ck0_ref, cv0_ref, ck1_ref, cv1_ref, ck2_ref, cv2_ref,
                        o_ref, *, n_new):
    scale = HEAD_DIM ** -0.5
    caches = ((ck0_ref, cv0_ref), (ck1_ref, cv1_ref), (ck2_ref, cv2_ref))
    jcol = lax.broadcasted_iota(jnp.int32, (SUBLANES, 1), 0)
    masks = []
    for ck_ref, _ in caches:
        rows = ck_ref.shape[-1]
        dil = rows // BAND
        qi = lax.broadcasted_iota(jnp.int32, (SUBLANES, rows), 0)
        ri = lax.broadcasted_iota(jnp.int32, (SUBLANES, rows), 1)
        masks.append(ri >= qi if dil == 1 else ri % dil == qi)

    def head(h, carry):
        sc, sn = [], []
        for g, (ck_ref, _) in enumerate(caches):
            q = q_ref[0, g, h]
            s = jnp.dot(q.astype(BF16), ck_ref[0, 0, h].astype(BF16), preferred_element_type=F32) * scale
            sc.append(jnp.where(masks[g], s, -jnp.inf))
            kn = kn_ref[0, g, h]
            for i in range(n_new):
                si = jnp.sum(q * kn[i:i + 1, :], axis=-1, keepdims=True) * scale
                sn.append(jnp.where(jcol >= i if g == 0 else jcol == i, si, -jnp.inf))
        m = functools.reduce(jnp.maximum, [jnp.max(s, axis=-1, keepdims=True) for s in sc] + sn)
        den = jnp.zeros((SUBLANES, 1), F32)
        acc = jnp.zeros((SUBLANES, HEAD_DIM), F32)
        for g, (_, cv_ref) in enumerate(caches):
            e = jnp.exp(sc[g] - m)
            den = den + jnp.sum(e, axis=-1, keepdims=True)
            acc = acc + lax.dot_general(e.astype(BF16), cv_ref[0, 0, h].astype(BF16),
                                        (((1,), (1,)), ((), ())), preferred_element_type=F32)
            vn = vn_ref[0, g, h]
            for i in range(n_new):
                en = jnp.exp(sn[g * n_new + i] - m)
                den = den + en
                acc = acc + en * vn[i:i + 1, :]
        o_ref[0, h] = acc / den
        return carry

    def head_pair(hh, carry):
        head(2 * hh, carry)
        return head(2 * hh + 1, carry)

    lax.fori_loop(0, N_HEADS // 2, head_pair, 0)


def sample_attn(q, kn, vn, caches_k, caches_v, n_new):
    B = q.shape[0]
    new_spec = pl.BlockSpec((1, N_GROUPS, N_HEADS, SUBLANES, HEAD_DIM), lambda b: (b, 0, 0, 0, 0))
    specs = [new_spec] * 3
    args = [q, kn, vn]
    for g in range(N_GROUPS):
        rows = caches_k[g].shape[2]
        assert rows == BAND * DILATIONS[g] and n_new <= SUBLANES and (g == 0 or n_new <= DILATIONS[g])
        spec = pl.BlockSpec((1, 1, N_HEADS, HEAD_DIM, rows), lambda b: (0, b, 0, 0, 0))
        specs += [spec, spec]
        args += [jnp.transpose(caches_k[g], (0, 1, 3, 4, 2)), jnp.transpose(caches_v[g], (0, 1, 3, 4, 2))]
    return pl.pallas_call(
        functools.partial(_sample_attn_kernel, n_new=n_new),
        grid=(B,),
        in_specs=specs,
        out_specs=pl.BlockSpec((1, N_HEADS, SUBLANES, HEAD_DIM), lambda b: (b, 0, 0, 0)),
        out_shape=jax.ShapeDtypeStruct((B, N_HEADS, SUBLANES, HEAD_DIM), F32),
        compiler_params=_params("parallel"),
        name="sample_attn",
    )(*args)


def _ffn_kernel(xp_ref, xs_ref, halo_ref, g_ref, wg_ref, wv_ref, cw_ref, cb_ref, wd_ref, wdl_ref, e1_ref, e2_ref,
                gf_ref, op_ref, os_ref, hn_scr, hh_scr, acc_scr, act_scr, *, seq_tiles, sample_tile, seq_new,
                final_norm):
    i = pl.program_id(0)
    j = pl.program_id(1)
    is_sample = i == sample_tile

    def norm(x):
        return (x * lax.rsqrt(jnp.mean(x * x, axis=-1, keepdims=True) + NORM_EPS) * g_ref[...]).astype(BF16)

    def x_tile():
        return jnp.where(is_sample, xs_ref[...], xp_ref[...])

    @pl.when(j == 0)
    def _():
        hn_scr[...] = norm(x_tile())
        hh_scr[...] = norm(halo_ref[...]).astype(F32)
        acc_scr[...] = jnp.zeros_like(acc_scr)
        act_scr[...] = jnp.zeros_like(act_scr)

    acc_scr[...] += jnp.dot(act_scr[...], wdl_ref[...], preferred_element_type=F32)

    hn = hn_scr[...]
    gate = jnp.dot(hn, wg_ref[...], preferred_element_type=F32)
    val = jnp.dot(hn, wv_ref[...], preferred_element_type=F32)
    gh = jnp.dot(hh_scr[...].astype(BF16), wg_ref[...], preferred_element_type=F32)
    gh = jnp.where(i % seq_tiles == 0, 0.0, gh)
    row = lax.broadcasted_iota(jnp.int32, gate.shape, 0)
    roll1 = pltpu.roll(gate, 1, axis=0)
    roll2 = pltpu.roll(gate, 2, axis=0)
    pos = row % seq_new
    prev1 = jnp.where(is_sample,
                      jnp.where(pos == 0, e1_ref[...], roll1),
                      jnp.where(row == 0, gh[7:8], roll1))
    prev2 = jnp.where(is_sample,
                      jnp.where(pos < 2, e2_ref[...], roll2),
                      jnp.where(row == 0, gh[6:7], jnp.where(row == 1, gh[7:8], roll2)))
    conv = cb_ref[...] + prev2 * cw_ref[0:1] + prev1 * cw_ref[1:2] + gate * cw_ref[2:3]
    act = (conv * jax.nn.sigmoid(conv) * val).astype(BF16)
    act_scr[...] = act

    @pl.when(j == pl.num_programs(1) - 1)
    def _():
        y = x_tile() + acc_scr[...] + jnp.dot(act_scr[...], wd_ref[...], preferred_element_type=F32)
        if final_norm:
            y = y * lax.rsqrt(jnp.mean(y * y, axis=-1, keepdims=True) + NORM_EPS) * gf_ref[...]

        @pl.when(is_sample)
        def _():
            os_ref[...] = y

        @pl.when(jnp.logical_not(is_sample))
        def _():
            op_ref[...] = y


def conv_glu_ffn(xp, xs, g, w_up, conv_w, conv_b, w_down, e1, e2, g_final, seq_len, seq_new, tm=512, tf=256):
    Mp, D = xp.shape
    F = w_down.shape[0]
    assert Mp % tm == 0 and F % tf == 0 and seq_len % tm == 0 and tm % seq_new == 0
    assert xs.shape == (tm, D) and e1.shape == (tm, F) and e2.shape == (tm, F)
    nf = F // tf
    sample_tile = Mp // tm
    prow = pl.BlockSpec((tm, D), lambda i, j: (jnp.minimum(i, sample_tile - 1), 0))
    srow = pl.BlockSpec((tm, D), lambda i, j: (0, 0))
    halo = pl.BlockSpec((SUBLANES, D),
                        lambda i, j: (jnp.clip(i * (tm // SUBLANES) - 1, 0, Mp // SUBLANES - 1), 0))
    vec = pl.BlockSpec((1, D), lambda i, j: (0, 0))
    fill = pl.BlockSpec((tm, tf), lambda i, j: (0, jnp.where(i == sample_tile, j, 0)))
    w_up = w_up.astype(BF16)
    w_down = w_down.astype(BF16)
    return pl.pallas_call(
        functools.partial(_ffn_kernel, seq_tiles=seq_len // tm, sample_tile=sample_tile, seq_new=seq_new,
                          final_norm=g_final is not None),
        grid=(sample_tile + 1, nf),
        in_specs=[prow, srow, halo, vec,
                  pl.BlockSpec((D, tf), lambda i, j: (0, j)),
                  pl.BlockSpec((D, tf), lambda i, j: (0, j + nf)),
                  pl.BlockSpec((CONV_W, tf), lambda i, j: (0, j)),
                  pl.BlockSpec((1, tf), lambda i, j: (0, j)),
                  pl.BlockSpec((tf, D), lambda i, j: (j, 0)),
                  pl.BlockSpec((tf, D), lambda i, j: (jnp.maximum(j - 1, 0), 0)),
                  fill, fill, vec],
        out_specs=[prow, srow],
        out_shape=[jax.ShapeDtypeStruct((Mp, D), F32), jax.ShapeDtypeStruct((tm, D), F32)],
        scratch_shapes=[pltpu.VMEM((tm, D), BF16), pltpu.VMEM((SUBLANES, D), F32), pltpu.VMEM((tm, D), F32),
                        pltpu.VMEM((tm, tf), BF16)],
        compiler_params=_params("arbitrary", "arbitrary"),
        name="conv_glu_ffn",
    )(xp, xs, xp, g.reshape(1, D), w_up, w_up, conv_w, conv_b.reshape(1, F), w_down, w_down, e1, e2,
      (g if g_final is None else g_final).reshape(1, D))


def _rmsnorm(x, g):
    return x * lax.rsqrt(jnp.mean(x * x, axis=-1, keepdims=True) + NORM_EPS) * g


def _rope_angles(pos):
    half = HEAD_DIM // 2
    inv = jnp.exp(-math.log(ROPE_THETA) * jnp.arange(half, dtype=F32) * 2.0 / HEAD_DIM)
    return pos.astype(F32)[:, None] * inv[None, :]


def _rope_tables(pos):
    ang = _rope_angles(pos)
    cos, sin = jnp.cos(ang), jnp.sin(ang)
    return jnp.tile(cos, (1, LANES // cos.shape[1])), jnp.tile(jnp.concatenate([-sin, sin], axis=1), (1, 2))


def _rope(x, pos):
    half = HEAD_DIM // 2
    ang = _rope_angles(pos)
    cos = jnp.cos(ang)[:, None, None, :]
    sin = jnp.sin(ang)[:, None, None, :]
    x1, x2 = x[..., :half], x[..., half:]
    return jnp.concatenate([x1 * cos - x2 * sin, x2 * cos + x1 * sin], axis=-1)


def kernel(x_prompt, x_sample, state_rwkv_shift, state_rwkv_wkv, cache_k_w128, cache_v_w128, cache_k_w512, cache_v_w512, cache_k_w2048, cache_v_w2048, state_ffn_conv, norm_mix, norm_ffn, norm_final, rwkv_mu, rwkv_w_rkv, rwkv_w0, rwkv_w1, rwkv_w2, rwkv_a0, rwkv_a1, rwkv_a2, rwkv_g1, rwkv_g2, rwkv_k_k, rwkv_k_a, rwkv_r_k, rwkv_ln_w, rwkv_ln_b, rwkv_w_o, attn_w_in, attn_w_o, ffn_w_up, ffn_conv_w, ffn_conv_b, ffn_w_down):
    Bp, Tp, D = x_prompt.shape
    Bs, Ts, _ = x_sample.shape
    Mp, Ms = Bp * Tp, Bs * Ts
    assert Ts >= CONV_W - 1 and Tp >= CONV_W - 1 and CONV_W == 3
    xp = x_prompt.reshape(Mp, D)
    xs = x_sample.reshape(Ms, D)
    caches_k = (cache_k_w128, cache_k_w512, cache_k_w2048)
    caches_v = (cache_v_w128, cache_v_w512, cache_v_w2048)

    p_shift, p_wkv, s_shift, s_wkv = [], [], [], []
    p_k = [[] for _ in range(N_GROUPS)]
    p_v = [[] for _ in range(N_GROUPS)]
    s_k = [[] for _ in range(N_GROUPS)]
    s_v = [[] for _ in range(N_GROUPS)]
    p_conv, s_conv = [], []

    def time_mix(i, li, x, carried, s0, B, T):
        vec = jnp.stack([rwkv_w0[li], rwkv_a0[li], rwkv_k_k[li], rwkv_k_a[li], rwkv_r_k[li].reshape(D)])
        outs = rwkv_prep(
            x, x if carried is None else carried, norm_mix[i], rwkv_mu[li], rwkv_w_rkv[li], rwkv_w1[li], rwkv_w2[li],
            rwkv_a1[li], rwkv_a2[li], rwkv_g1[li], rwkv_g2[li], vec, T, 0 if carried is None else T)
        if carried is None:
            v, gate, sc = outs[7:]
            y, st = wkv_scan_tokens(outs[:6], outs[6])
            y, st = y.reshape(B * T, D), _state_out(st, B)
        else:
            a, wr, w, b, k, v, gate, sc = outs
            y, st = _wkv_carried(a, wr, w, b, k, v, sc[:, :N_HEADS], sc[:, N_HEADS:2 * N_HEADS], s0, B, T)
        return rwkv_post(y, v, gate, sc, x, rwkv_w_o[li], rwkv_ln_w[li], rwkv_ln_b[li]), st

    depth = norm_mix.shape[0]
    for i in range(depth):
        li = i // 2
        if i % 2 == 0:
            p_shift.append(_rmsnorm(xp.reshape(Bp, Tp, D)[:, -1], norm_mix[i]))
            s_shift.append(_rmsnorm(xs.reshape(Bs, Ts, D)[:, -1], norm_mix[i]))
            xp, stp = time_mix(i, li, xp, None, None, Bp, Tp)
            xs, sts = time_mix(i, li, xs, jnp.repeat(state_rwkv_shift[li], Ts, axis=0), state_rwkv_wkv[li], Bs, Ts)
            p_wkv.append(stp)
            s_wkv.append(sts)
        else:
            hs = _rmsnorm(xs, norm_mix[i])
            assert len(caches_k[0].shape) == 5 and caches_k[0].shape[0] == 1 and li == 0
            hgs = norm_classes(xp, norm_mix[i], DILATIONS, Tp)
            outs, lses = [], []
            for g in range(N_GROUPS):
                dil = DILATIONS[g]
                L = Tp // dil
                keep = min(WINDOWS[g], Tp)
                assert Tp % (dil * BAND) == 0 and keep % dil == 0
                qkv_g = mm(hgs[g], attn_w_in[li][:, g * 3 * D:(g + 1) * 3 * D]).reshape(Bp * dil, L, 3 * D)
                pos = (jnp.arange(dil)[:, None] + dil * jnp.arange(L)[None, :]).reshape(-1)
                o, lse, kr = band_attn(qkv_g, *_rope_tables(pos), dil)
                outs.append(o)
                lses.append(lse)
                tail = lambda t: t.reshape(Bp, dil, L, D)[:, :, L - keep // dil:].transpose(0, 2, 1, 3).reshape(
                    Bp, keep, N_HEADS, HEAD_DIM)
                p_k[g].append(tail(kr))
                p_v[g].append(tail(qkv_g[..., 2 * D:]))
            qkv_s = mm(hs.astype(BF16), attn_w_in[li]).reshape(Bs, Ts, N_GROUPS, 3, N_HEADS, HEAD_DIM)
            pos_s = PAST_LEN + jnp.arange(Ts)
            q_s = _rope(qkv_s[:, :, :, 0], pos_s)
            k_s = _rope(qkv_s[:, :, :, 1], pos_s)
            v_s = qkv_s[:, :, :, 2]
            rows8 = lambda t: jnp.pad(t.transpose(0, 2, 3, 1, 4), ((0, 0),) * 3 + ((0, SUBLANES - Ts), (0, 0)))
            osmp = sample_attn(rows8(q_s), rows8(k_s), rows8(v_s), caches_k, caches_v, Ts)
            osmp = osmp[:, :, :Ts].transpose(0, 2, 1, 3).reshape(Ms, D)
            for g in range(N_GROUPS):
                s_k[g].append(k_s[:, :, g])
                s_v[g].append(v_s[:, :, g])
            xp = attn_merge(outs, lses, DILATIONS, xp, attn_w_o[li], Tp)
            xs = xs + mm(osmp, attn_w_o[li])

        buf = state_ffn_conv[i]
        first = (jnp.arange(Ms) % Ts == 0)[:, None]
        e1 = jnp.repeat(buf[:, 1], Ts, axis=0)
        e2 = jnp.where(first, jnp.repeat(buf[:, 0], Ts, axis=0), e1)
        tail = jnp.concatenate([xp.reshape(Bp, Tp, D)[:, Tp - 2:].reshape(Bp * 2, D),
                                xs.reshape(Bs, Ts, D)[:, Ts - 2:].reshape(Bs * 2, D)], axis=0)
        gate_tail = mm(_rmsnorm(tail, norm_ffn[i]), ffn_w_up[i][:, :D_FF])
        p_conv.append(gate_tail[:Bp * 2].reshape(Bp, 2, D_FF))
        s_conv.append(gate_tail[Bp * 2:].reshape(Bs, 2, D_FF))
        xp, xs = conv_glu_ffn(xp, xs, norm_ffn[i], ffn_w_up[i], ffn_conv_w[i], ffn_conv_b[i], ffn_w_down[i],
                              e1, e2, norm_final if i == depth - 1 else None, Tp, Ts)

    y_prompt = xp.reshape(Bp, Tp, D)
    y_sample = xs.reshape(Bs, Ts, D)
    st = jnp.stack
    return (y_prompt, y_sample,
            st(p_shift), st(p_wkv),
            st(p_k[0]), st(p_v[0]), st(p_k[1]), st(p_v[1]), st(p_k[2]), st(p_v[2]), st(p_conv),
            st(s_shift), st(s_wkv),
            st(s_k[0]), st(s_v[0]), st(s_k[1]), st(s_v[1]), st(s_k[2]), st(s_v[2]), st(s_conv))
```

```python
import functools
import math

import jax
import jax.numpy as jnp
from jax import lax
from jax.experimental import pallas as pl
from jax.experimental.pallas import tpu as pltpu

F32 = jnp.float32
BF16 = jnp.bfloat16

D_MODEL = 1024
HEAD_DIM = 64
N_HEADS = 16
D_FF = 2816
CONV_W = 3
N_GROUPS = 3
WINDOWS = (128, 512, 2048)
DILATIONS = (1, 4, 16)
BAND = 128
GN_EPS = 64e-5
NORM_EPS = 1e-6
ROPE_THETA = 10000.0
PAST_LEN = 2048
LANES = 128
SUBLANES = 8
VMEM_LIMIT = 56 * 1024 * 1024
SCAN_VMEM_LIMIT = 62 * 1024 * 1024


def _params(*sem):
    return pltpu.CompilerParams(dimension_semantics=sem, vmem_limit_bytes=VMEM_LIMIT)


def _mm_kernel(x_ref, w_ref, o_ref):
    o_ref[...] = jnp.dot(x_ref[...].astype(BF16), w_ref[...],
                         preferred_element_type=F32).astype(o_ref.dtype)


def mm(x, w, tm=1024, tn=1024):
    M, K = x.shape
    N = w.shape[1]
    tm = next(t for t in (tm, tm // 2, M) if M % t == 0)
    tn = next(t for t in (tn, tn // 2, N) if N % t == 0)
    return pl.pallas_call(
        _mm_kernel,
        grid=(M // tm, N // tn),
        in_specs=[pl.BlockSpec((tm, K), lambda i, j: (i, 0)),
                  pl.BlockSpec((K, tn), lambda i, j: (0, j))],
        out_specs=pl.BlockSpec((tm, tn), lambda i, j: (i, j)),
        out_shape=jax.ShapeDtypeStruct((M, N), F32),
        compiler_params=_params("parallel", "parallel"),
        name="mm",
    )(x, w.astype(BF16))


VALUE_ROWS = 4


def _wkv_state_kernel(a_ref, wr_ref, w_ref, b_ref, k_ref, v_ref, br_ref, kr_ref, s_ref, y_ref, so_ref, *, steps):
    def rows(c, carry):
        v0 = pl.multiple_of(c * VALUE_ROWS, VALUE_ROWS)
        s = s_ref[0, pl.ds(v0, VALUE_ROWS)]
        for t in range(steps):
            sa = jnp.sum(s * a_ref[t, 0][None], axis=1, keepdims=True)
            yy = jnp.sum(s * wr_ref[t, 0][None], axis=1, keepdims=True)
            v = v_ref[t, 0, pl.ds(v0, VALUE_ROWS), :][:, None, :]
            y = yy + sa * br_ref[t, 0][None] + v * kr_ref[t, 0][None]
            y_ref[t, 0, pl.ds(v0, VALUE_ROWS), :] = y[:, 0, :]
            s = s * w_ref[t, 0][None] + sa * b_ref[t, 0][None] + v * k_ref[t, 0][None]
        so_ref[0, pl.ds(v0, VALUE_ROWS)] = s
        return carry

    lax.fori_loop(0, HEAD_DIM // VALUE_ROWS, rows, 0)


def wkv_scan_state(ops, br, kr, s0):
    T, H, dh, B = ops[0].shape
    op_spec = pl.BlockSpec((T, 1, dh, B), lambda h: (0, h, 0, 0))
    sc_spec = pl.BlockSpec((T, 1, 1, B), lambda h: (0, h, 0, 0))
    st_spec = pl.BlockSpec((1, dh, dh, B), lambda h: (h, 0, 0, 0))
    return pl.pallas_call(
        functools.partial(_wkv_state_kernel, steps=T),
        grid=(H,),
        in_specs=[op_spec] * 6 + [sc_spec, sc_spec, st_spec],
        out_specs=[op_spec, st_spec],
        out_shape=[jax.ShapeDtypeStruct((T, H, dh, B), F32), jax.ShapeDtypeStruct(s0.shape, F32)],
        compiler_params=_params("parallel"),
        name="wkv_scan_state",
    )(*ops, br, kr, s0)


CHUNK = LANES
KEYS_PER_TRIP = 32


def _wkv_chunk_kernel(a_ref, wr_ref, w_ref, b_ref, k_ref, v_ref, sc_ref, y_ref, st_ref,
                      state, zrow, zv, zs, zy, xt):
    c = pl.program_id(0)
    nb = a_ref.shape[0]
    rows_bh = nb * N_HEADS

    @pl.when(c == 0)
    def _():
        state[...] = jnp.zeros_like(state)

    def lanes_tile(ref, first, second, stride):
        top = ref[:, pl.ds(first, N_HEADS, stride=stride), :].reshape(rows_bh, CHUNK)
        bot = ref[:, pl.ds(second, N_HEADS, stride=stride), :].reshape(rows_bh, CHUNK)
        return jnp.concatenate([top, bot], axis=0).T

    def grouped(n, body):
        def trip(g, carry):
            for u in range(SUBLANES):
                body(g * SUBLANES + u)
            return carry
        lax.fori_loop(0, n // SUBLANES, trip, 0)

    for op, ref in enumerate((a_ref, wr_ref, w_ref, b_ref, k_ref)):
        def fill(k, op=op, ref=ref):
            zrow[op, k] = lanes_tile(ref, k, k, HEAD_DIM)
        grouped(HEAD_DIM, fill)

    def fill_v(v_lo):
        zv[pl.ds(pl.multiple_of(v_lo * CHUNK, CHUNK), CHUNK), :] = lanes_tile(
            v_ref, v_lo, HEAD_DIM // 2 + v_lo, HEAD_DIM)
    grouped(HEAD_DIM // 2, fill_v)

    zs[0] = lanes_tile(sc_ref, 0, 0, 1)
    zs[1] = lanes_tile(sc_ref, N_HEADS, N_HEADS, 1)

    def row(op, k, t):
        return jnp.broadcast_to(zrow[op, k, pl.ds(t, 1), :], (SUBLANES, LANES))[None]

    def step(t, carry):
        def reduce_keys(g, acc):
            sa, yy = acc
            for u in range(KEYS_PER_TRIP):
                k = g * KEYS_PER_TRIP + u
                sk = state[k]
                sa = sa + sk * row(0, k, t)
                yy = yy + sk * row(1, k, t)
            return sa, yy
        zero = jnp.zeros((4, SUBLANES, LANES), F32)
        sa, yy = lax.fori_loop(0, HEAD_DIM // KEYS_PER_TRIP, reduce_keys, (zero, zero))
        v = zv[pl.ds(t, HEAD_DIM // 2, stride=CHUNK), :].reshape(4, SUBLANES, LANES)
        br = jnp.broadcast_to(zs[0, pl.ds(t, 1), :], (SUBLANES, LANES))[None]
        kr = jnp.broadcast_to(zs[1, pl.ds(t, 1), :], (SUBLANES, LANES))[None]
        zy[pl.ds(t, HEAD_DIM // 2, stride=CHUNK), :] = (yy + sa * br + v * kr).reshape(HEAD_DIM // 2, LANES)

        def update_keys(g, carry):
            for u in range(KEYS_PER_TRIP):
                k = g * KEYS_PER_TRIP + u
                state[k] = state[k] * row(2, k, t) + sa * row(3, k, t) + v * row(4, k, t)
            return carry
        lax.fori_loop(0, HEAD_DIM // KEYS_PER_TRIP, update_keys, 0)
        return carry

    lax.fori_loop(0, CHUNK, step, 0)

    def drain(v_lo):
        tile = zy[pl.ds(pl.multiple_of(v_lo * CHUNK, CHUNK), CHUNK), :].T
        for v_hi in range(2):
            xt[:, pl.ds(v_hi * (HEAD_DIM // 2) + v_lo, N_HEADS, stride=HEAD_DIM), :] = (
                tile[v_hi * rows_bh:(v_hi + 1) * rows_bh].reshape(nb, N_HEADS, CHUNK))
    grouped(HEAD_DIM // 2, drain)

    def to_token_major(b, carry):
        for cb in range(y_ref.shape[2] // LANES):
            y_ref[b, :, cb * LANES:(cb + 1) * LANES] = xt[b, cb * LANES:(cb + 1) * LANES, :].T
        return carry
    lax.fori_loop(0, nb, to_token_major, 0)

    @pl.when(c == pl.num_programs(0) - 1)
    def _():
        st_ref[...] = state[...]


def wkv_scan_tokens(ops, sc):
    B, D, T = ops[0].shape
    assert B * N_HEADS * 2 == LANES and T % CHUNK == 0 and D == N_HEADS * HEAD_DIM
    spec = pl.BlockSpec((B, D, CHUNK), lambda c: (0, 0, c))
    st_shape = (HEAD_DIM, 4, SUBLANES, LANES)
    return pl.pallas_call(
        _wkv_chunk_kernel,
        grid=(T // CHUNK,),
        in_specs=[spec] * 6 + [pl.BlockSpec((B, LANES, CHUNK), lambda c: (0, 0, c))],
        out_specs=[pl.BlockSpec((B, CHUNK, D), lambda c: (0, c, 0)), pl.BlockSpec(st_shape, lambda c: (0, 0, 0, 0))],
        out_shape=[jax.ShapeDtypeStruct((B, T, D), F32), jax.ShapeDtypeStruct(st_shape, F32)],
        scratch_shapes=[pltpu.VMEM(st_shape, F32),
                        pltpu.VMEM((5, HEAD_DIM, CHUNK, LANES), F32),
                        pltpu.VMEM((HEAD_DIM // 2 * CHUNK, LANES), F32),
                        pltpu.VMEM((2, CHUNK, LANES), F32),
                        pltpu.VMEM((HEAD_DIM // 2 * CHUNK, LANES), F32),
                        pltpu.VMEM((B, D, CHUNK), F32)],
        compiler_params=pltpu.CompilerParams(dimension_semantics=("arbitrary",), vmem_limit_bytes=SCAN_VMEM_LIMIT),
        name="wkv_scan_tokens",
    )(*ops, sc)


def _state_out(s, B):
    nl = B * N_HEADS // 64
    s = s.reshape(HEAD_DIM, 32, nl, 2, 64).transpose(2, 4, 3, 1, 0)
    return s.reshape(B, N_HEADS, HEAD_DIM, HEAD_DIM)


def _wkv_carried(a, wr, w, b, k, v, br, kr, s0, B, T):
    lanes_b = lambda t, n: t.reshape(B, T, N_HEADS, n).transpose(1, 2, 3, 0)
    y, st = wkv_scan_state([lanes_b(t, HEAD_DIM) for t in (a, wr, w, b, k, v)], lanes_b(br, 1), lanes_b(kr, 1),
                           jnp.transpose(s0, (1, 2, 3, 0)))
    return y.transpose(3, 0, 1, 2).reshape(B * T, D_MODEL), jnp.transpose(st, (3, 0, 1, 2))


def _dot_split(x, m):
    hi = x.astype(BF16)
    lo = (x - hi.astype(F32)).astype(BF16)
    return jnp.dot(hi, m, preferred_element_type=F32) + jnp.dot(lo, m, preferred_element_type=F32)


def _head_matrices():
    lane_head = jnp.arange(D_MODEL)[:, None] // HEAD_DIM
    col = jnp.arange(LANES)[None, :]
    seg = jnp.stack([(col == lane_head + N_HEADS * j) for j in range(3)]).astype(BF16)
    return seg, jnp.swapaxes(seg, 1, 2)


def _rwkv_prep_kernel(x_ref, halo_ref, e_ref, g_ref, mu_ref, wrkv_ref, w1_ref, w2_ref, a1_ref, a2_ref, g1_ref,
                      g2_ref, vec_ref, seg_ref, spread_ref, *outs, seq_tiles, seq_new):
    i = pl.program_id(0)

    def norm(x):
        return x * lax.rsqrt(jnp.mean(x * x, axis=-1, keepdims=True) + NORM_EPS) * g_ref[...]

    h = norm(x_ref[...])
    row = lax.broadcasted_iota(jnp.int32, h.shape, 0)
    shifted = pltpu.roll(h, 1, axis=0)
    if seq_new:
        prev = jnp.where(row % seq_new == 0, e_ref[...], shifted)
    else:
        before = jnp.where(i % seq_tiles == 0, 0.0, norm(halo_ref[...])[SUBLANES - 1:])
        prev = jnp.where(row == 0, before, shifted)
    xx = prev - h
    mix = lambda n: (h + xx * mu_ref[n:n + 1]).astype(BF16)
    proj = lambda x, w: jnp.dot(x, w, preferred_element_type=F32)
    w0, a0, k_k, k_a, r_k = (vec_ref[n:n + 1] for n in range(5))
    r = proj(mix(0), wrkv_ref[0])
    k = proj(mix(1), wrkv_ref[1])
    v = proj(mix(2), wrkv_ref[2])
    z = -(w0 + proj(jnp.tanh(proj(mix(3), w1_ref[...])).astype(BF16), w2_ref[...]))
    softplus = jnp.maximum(z, 0.0) + jnp.log(1.0 + jnp.exp(-jnp.abs(z)))
    w = jnp.exp(-jnp.exp(-softplus - 0.5))
    a = jax.nn.sigmoid(a0 + proj(proj(mix(4), a1_ref[...]).astype(BF16), a2_ref[...]))
    gate = proj(jax.nn.sigmoid(proj(mix(5), g1_ref[...])).astype(BF16), g2_ref[...])
    kk = k * k_k
    norm2 = _dot_split(_dot_split(kk * kk, seg_ref[0]), spread_ref[0])
    kk = kk / jnp.maximum(jnp.sqrt(norm2), 1e-12)
    k = k * (1.0 + (a - 1.0) * k_a)
    b = kk * a
    sc = _dot_split(b * r, seg_ref[0]) + _dot_split(k * r, seg_ref[1]) + _dot_split(r * k * r_k, seg_ref[2])
    scan_ops = (-kk, w * r, w, b, k, v)
    if seq_new:
        for ref, val in zip(outs[:6], scan_ops):
            ref[...] = val
        gate_out, sc_out = outs[6:]
    else:
        for ref, val in zip(outs[:6], scan_ops):
            ref[0] = val.T
        outs[6][0] = sc.T
        v_out, gate_out, sc_out = outs[7:]
        v_out[...] = v
    gate_out[...] = gate
    sc_out[...] = sc


def rwkv_prep(x, e, g, mu, w_rkv, w1, w2, a1, a2, g1, g2, vec, seq_len, seq_new, tm=256):
    M, D = x.shape
    assert M % tm == 0 and (tm % seq_new == 0 if seq_new else seq_len % tm == 0)
    seg, spread = _head_matrices()
    once = pl.Buffered(1)
    row = pl.BlockSpec((tm, D), lambda i: (i, 0))
    halo = pl.BlockSpec((SUBLANES, D), lambda i: (jnp.maximum(i * (tm // SUBLANES) - 1, 0), 0))
    full = lambda t: pl.BlockSpec(t.shape, lambda i: (0,) * t.ndim, pipeline_mode=once)
    consts = [g.reshape(1, D), mu] + [t.astype(BF16) for t in (w_rkv, w1, w2, a1, a2, g1, g2)] + [vec, seg, spread]
    out = jax.ShapeDtypeStruct((M, D), F32)
    packed = pl.BlockSpec((tm, LANES), lambda i: (i, 0))
    packed_out = jax.ShapeDtypeStruct((M, LANES), F32)
    if seq_new:
        out_specs = [row] * 7 + [packed]
        out_shape = [out] * 7 + [packed_out]
    else:
        per_seq = seq_len // tm
        cm = lambda n: pl.BlockSpec((1, n, tm), lambda i: (i // per_seq, 0, i % per_seq))
        cm_out = lambda n: jax.ShapeDtypeStruct((M // seq_len, n, seq_len), F32)
        out_specs = [cm(D)] * 6 + [cm(LANES), row, row, packed]
        out_shape = [cm_out(D)] * 6 + [cm_out(LANES), out, out, packed_out]
    return pl.pallas_call(
        functools.partial(_rwkv_prep_kernel, seq_tiles=max(seq_len // tm, 1), seq_new=seq_new),
        grid=(M // tm,),
        in_specs=[row, halo, row if seq_new else pl.BlockSpec((tm, D), lambda i: (0, 0))] + [full(t) for t in consts],
        out_specs=out_specs,
        out_shape=out_shape,
        compiler_params=_params("parallel"),
        name="rwkv_prep",
    )(x, x, e, *consts)


def _rwkv_post_kernel(y_ref, v_ref, gate_ref, sc_ref, x_ref, wo_ref, lnw_ref, lnb_ref, seg_ref, spread_ref, o_ref):
    y = y_ref[...]
    inv = 1.0 / HEAD_DIM
    mean = _dot_split(_dot_split(y, seg_ref[0]), spread_ref[0]) * inv
    d = y - mean
    var = _dot_split(_dot_split(d * d, seg_ref[0]), spread_ref[0]) * inv
    yn = d * lax.rsqrt(var + GN_EPS) * lnw_ref[...] + lnb_ref[...]
    bonus = _dot_split(sc_ref[...], spread_ref[2]) * v_ref[...]
    out = ((yn + bonus) * gate_ref[...]).astype(BF16)
    o_ref[...] = x_ref[...] + jnp.dot(out, wo_ref[...], preferred_element_type=F32)


def rwkv_post(y, v, gate, sc, x, w_o, ln_w, ln_b, tm=512):
    M, D = x.shape
    assert M % tm == 0
    seg, spread = _head_matrices()
    once = pl.Buffered(1)
    row = pl.BlockSpec((tm, D), lambda i: (i, 0))
    full = lambda t: pl.BlockSpec(t.shape, lambda i: (0,) * t.ndim, pipeline_mode=once)
    consts = [w_o.astype(BF16), ln_w.reshape(1, D), ln_b.reshape(1, D), seg, spread]
    return pl.pallas_call(
        _rwkv_post_kernel,
        grid=(M // tm,),
        in_specs=[row, row, row, pl.BlockSpec((tm, LANES), lambda i: (i, 0)), row] + [full(t) for t in consts],
        out_specs=row,
        out_shape=jax.ShapeDtypeStruct((M, D), F32),
        compiler_params=_params("parallel"),
        name="rwkv_post",
    )(y, v, gate, sc, x, *consts)


def _band_attn_kernel(q_ref, k_ref, v_ref, cos_ref, sin_ref, o_ref, lse_ref, kr_ref, kprev, vprev):
    n = pl.program_id(1)

    @pl.when(n == 0)
    def _():
        kprev[...] = jnp.zeros_like(kprev)
        vprev[...] = jnp.zeros_like(vprev)

    i = lax.broadcasted_iota(jnp.int32, (BAND, 2 * BAND), 0)
    j = lax.broadcasted_iota(jnp.int32, (BAND, 2 * BAND), 1)
    valid = (j >= i) & (j <= i + BAND) & ((n > 0) | (j >= BAND))
    lane = lax.broadcasted_iota(jnp.int32, (BAND, LANES), 1)
    first_head = lane < HEAD_DIM
    first_half = lane % HEAD_DIM < HEAD_DIM // 2
    cos, sin = cos_ref[...], sin_ref[...]

    def rope(x):
        partner = jnp.where(first_half, pltpu.roll(x, LANES - HEAD_DIM // 2, axis=1),
                            pltpu.roll(x, HEAD_DIM // 2, axis=1))
        return x * cos + partner * sin

    k_before, v_before = kprev[...], vprev[...]
    key_head = lax.broadcasted_iota(jnp.int32, (2 * BAND, LANES), 1) < HEAD_DIM
    k_now, v_now = [], []
    for p in range(N_HEADS // 2):
        cs = slice(p * LANES, (p + 1) * LANES)
        q = rope(q_ref[0, :, cs])
        k = rope(k_ref[0, :, cs])
        kr_ref[0, :, cs] = k
        k_now.append(k.astype(BF16))
        v_now.append(v_ref[0, :, cs].astype(BF16))
        kcat = jnp.concatenate([k_before[:, cs], k_now[-1]], axis=0)
        vcat = jnp.concatenate([v_before[:, cs], v_now[-1]], axis=0)
        tops, sums = [], []
        for head_mask, own in ((first_head, key_head), (jnp.logical_not(first_head), jnp.logical_not(key_head))):
            qm = jnp.where(head_mask, q, 0.0).astype(BF16)
            s = lax.dot_general(qm, kcat, (((1,), (1,)), ((), ())), preferred_element_type=F32)
            s = jnp.where(valid, s * (HEAD_DIM ** -0.5), -jnp.inf)
            m = jnp.max(s, axis=-1, keepdims=True)
            e = jnp.exp(s - m).astype(BF16)
            sums.append(jnp.dot(e, jnp.where(own, vcat, 1.0), preferred_element_type=F32))
            tops.append(m)
        den = pltpu.roll(jnp.where(first_head, sums[1], sums[0]), HEAD_DIM, axis=1)
        o_ref[0, :, cs] = jnp.where(first_head, sums[0], sums[1]) / den
        lse_ref[0, :, cs] = jnp.where(first_head, tops[0], tops[1]) + jnp.log(den)
    kprev[...] = jnp.concatenate(k_now, axis=1)
    vprev[...] = jnp.concatenate(v_now, axis=1)


def band_attn(qkv, cos, sin, dil):
    S, L, W = qkv.shape
    D = W // 3
    nb = L // BAND
    col = lambda m: pl.BlockSpec((1, BAND, D), lambda s, n: (s, n, m))
    tab = pl.BlockSpec((BAND, LANES), lambda s, n: ((s % dil) * nb + n, 0))
    out = jax.ShapeDtypeStruct((S, L, D), F32)
    return pl.pallas_call(
        _band_attn_kernel,
        grid=(S, nb),
        in_specs=[col(0), col(1), col(2), tab, tab],
        out_specs=[col(0)] * 3,
        out_shape=[out] * 3,
        scratch_shapes=[pltpu.VMEM((BAND, D), BF16), pltpu.VMEM((BAND, D), BF16)],
        compiler_params=_params("parallel", "arbitrary"),
        name="band_attn",
    )(qkv, qkv, qkv, cos, sin)


def _norm_classes_kernel(x_ref, g_ref, *refs, dils):
    outs, scr = refs[:len(dils)], refs[len(dils)]
    x = x_ref[...]
    h = x * lax.rsqrt(jnp.mean(x * x, axis=-1, keepdims=True) + NORM_EPS) * g_ref[...]
    for j in range(scr.shape[0]):
        scr[j] = h[:, j * LANES:(j + 1) * LANES]
    for ref, dil in zip(outs, dils):
        if dil == 1:
            ref[0, 0] = h.astype(BF16)
            continue
        for c in range(dil):
            for j in range(scr.shape[0]):
                ref[0, c, :, j * LANES:(j + 1) * LANES] = scr[j, pl.ds(c, ref.shape[2], stride=dil), :].astype(BF16)


def norm_classes(x, g, dils, seq_len, tm=256):
    M, D = x.shape
    assert seq_len % tm == 0 and all(tm % (d * 2 * SUBLANES) == 0 for d in dils)
    per_seq = seq_len // tm
    spec = lambda d: pl.BlockSpec((1, d, tm // d, D), lambda i: (i // per_seq, 0, i % per_seq, 0))
    shape = lambda d: jax.ShapeDtypeStruct((M // seq_len, d, seq_len // d, D), BF16)
    outs = pl.pallas_call(
        functools.partial(_norm_classes_kernel, dils=tuple(dils)),
        grid=(M // tm,),
        in_specs=[pl.BlockSpec((tm, D), lambda i: (i, 0)), pl.BlockSpec((1, D), lambda i: (0, 0))],
        out_specs=[spec(d) for d in dils],
        out_shape=[shape(d) for d in dils],
        scratch_shapes=[pltpu.VMEM((D // LANES, tm, LANES), F32)],
        compiler_params=_params("parallel"),
        name="norm_classes",
    )(x, g.reshape(1, D))
    return [o.reshape(M, D) for o in outs]


def _attn_merge_kernel(*refs, dils):
    n = len(dils)
    o_refs, l_refs = refs[:n], refs[n:2 * n]
    x_ref, wo_ref, out_ref, scr = refs[2 * n:]

    def token_major(ref, dil):
        if dil == 1:
            return ref[0, 0]
        for c in range(dil):
            for j in range(scr.shape[0]):
                scr[j, pl.ds(c, ref.shape[2], stride=dil), :] = ref[0, c, :, j * LANES:(j + 1) * LANES]
        return jnp.concatenate([scr[j] for j in range(scr.shape[0])], axis=1)

    outs = [token_major(r, d) for r, d in zip(o_refs, dils)]
    lses = [token_major(r, d) for r, d in zip(l_refs, dils)]
    top = functools.reduce(jnp.maximum, lses)
    wts = [jnp.exp(l - top) for l in lses]
    merged = sum(w * o for w, o in zip(wts, outs)) / sum(wts)
    out_ref[...] = x_ref[...] + jnp.dot(merged.astype(BF16), wo_ref[...], preferred_element_type=F32)


def attn_merge(outs, lses, dils, x, w_o, seq_len, tm=256):
    M, D = x.shape
    assert seq_len % tm == 0 and all(tm % (d * SUBLANES) == 0 for d in dils)
    per_seq = seq_len // tm
    group = lambda t, d: t.reshape(M // seq_len, d, seq_len // d, D)
    spec = lambda d: pl.BlockSpec((1, d, tm // d, D), lambda i: (i // per_seq, 0, i % per_seq, 0))
    row = pl.BlockSpec((tm, D), lambda i: (i, 0))
    return pl.pallas_call(
        functools.partial(_attn_merge_kernel, dils=tuple(dils)),
        grid=(M // tm,),
        in_specs=[spec(d) for d in dils] * 2 + [row, pl.BlockSpec((D, D), lambda i: (0, 0))],
        out_specs=row,
        out_shape=jax.ShapeDtypeStruct((M, D), F32),
        scratch_shapes=[pltpu.VMEM((D // LANES, tm, LANES), F32)],
        compiler_params=_params("parallel"),
        name="attn_merge",
    )(*[group(t, d) for t, d in zip(outs, dils)], *[group(t, d) for t, d in zip(lses, dils)], x, w_o.astype(BF16))


def _sample_attn_kernel(q_ref, kn_ref, vn_ref, ck0_ref, cv0_ref, ck1_ref, cv1_ref, ck2_ref, cv2_ref,
                        o_ref, *, n_new):
    scale = HEAD_DIM ** -0.5
    caches = ((ck0_ref, cv0_ref), (ck1_ref, cv1_ref), (ck2_ref, cv2_ref))
    jcol = lax.broadcasted_iota(jnp.int32, (SUBLANES, 1), 0)
    masks = []
    for ck_ref, _ in caches:
        rows = ck_ref.shape[-1]
        dil = rows // BAND
        qi = lax.broadcasted_iota(jnp.int32, (SUBLANES, rows), 0)
        ri = lax.broadcasted_iota(jnp.int32, (SUBLANES, rows), 1)
        masks.append(ri >= qi if dil == 1 else ri % dil == qi)

    def head(h, carry):
        sc, sn = [], []
        for g, (ck_ref, _) in enumerate(caches):
            q = q_ref[0, g, h]
            s = jnp.dot(q.astype(BF16), ck_ref[0, 0, h].astype(BF16), preferred_element_type=F32) * scale
            sc.append(jnp.where(masks[g], s, -jnp.inf))
            kn = kn_ref[0, g, h]
            for i in range(n_new):
                si = jnp.sum(q * kn[i:i + 1, :], axis=-1, keepdims=True) * scale
                sn.append(jnp.where(jcol >= i if g == 0 else jcol == i, si, -jnp.inf))
        m = functools.reduce(jnp.maximum, [jnp.max(s, axis=-1, keepdims=True) for s in sc] + sn)
        den = jnp.zeros((SUBLANES, 1), F32)
        acc = jnp.zeros((SUBLANES, HEAD_DIM), F32)
        for g, (_, cv_ref) in enumerate(caches):
            e = jnp.exp(sc[g] - m)
            den = den + jnp.sum(e, axis=-1, keepdims=True)
            acc = acc + lax.dot_general(e.astype(BF16), cv_ref[0, 0, h].astype(BF16),
                                        (((1,), (1,)), ((), ())), preferred_element_type=F32)
            vn = vn_ref[0, g, h]
            for i in range(n_new):
                en = jnp.exp(sn[g * n_new + i] - m)
                den = den + en
                acc = acc + en * vn[i:i + 1, :]
        o_ref[0, h] = acc / den
        return carry

    def head_pair(hh, carry):
        head(2 * hh, carry)
        return head(2 * hh + 1, carry)

    lax.fori_loop(0, N_HEADS // 2, head_pair, 0)


def sample_attn(q, kn, vn, caches_k, caches_v, n_new):
    B = q.shape[0]
    new_spec = pl.BlockSpec((1, N_GROUPS, N_HEADS, SUBLANES, HEAD_DIM), lambda b: (b, 0, 0, 0, 0))
    specs = [new_spec] * 3
    args = [q, kn, vn]
    for g in range(N_GROUPS):
        rows = caches_k[g].shape[2]
        assert rows == BAND * DILATIONS[g] and n_new <= SUBLANES and (g == 0 or n_new <= DILATIONS[g])
        spec = pl.BlockSpec((1, 1, N_HEADS, HEAD_DIM, rows), lambda b: (0, b, 0, 0, 0))
        specs += [spec, spec]
        args += [jnp.transpose(caches_k[g], (0, 1, 3, 4, 2)), jnp.transpose(caches_v[g], (0, 1, 3, 4, 2))]
    return pl.pallas_call(
        functools.partial(_sample_attn_kernel, n_new=n_new),
        grid=(B,),
        in_specs=specs,
        out_specs=pl.BlockSpec((1, N_HEADS, SUBLANES, HEAD_DIM), lambda b: (b, 0, 0, 0)),
        out_shape=jax.ShapeDtypeStruct((B, N_HEADS, SUBLANES, HEAD_DIM), F32),
        compiler_params=_params("parallel"),
        name="sample_attn",
    )(*args)


def _ffn_kernel(xp_ref, xs_ref, halo_ref, g_ref, wg_ref, wv_ref, cw_ref, cb_ref, wd_ref, wdl_ref, e1_ref, e2_ref,
                gf_ref, op_ref, os_ref, hn_scr, hh_scr, acc_scr, act_scr, *, seq_tiles, sample_tile, seq_new,
                final_norm):
    i = pl.program_id(0)
    j = pl.program_id(1)
    is_sample = i == sample_tile

    def norm(x):
        return (x * lax.rsqrt(jnp.mean(x * x, axis=-1, keepdims=True) + NORM_EPS) * g_ref[...]).astype(BF16)

    def x_tile():
        return jnp.where(is_sample, xs_ref[...], xp_ref[...])

    @pl.when(j == 0)
    def _():
        hn_scr[...] = norm(x_tile())
        hh_scr[...] = norm(halo_ref[...]).astype(F32)
        acc_scr[...] = jnp.zeros_like(acc_scr)
        act_scr[...] = jnp.zeros_like(act_scr)

    acc_scr[...] += jnp.dot(act_scr[...], wdl_ref[...], preferred_element_type=F32)

    hn = hn_scr[...]
    gate = jnp.dot(hn, wg_ref[...], preferred_element_type=F32)
    val = jnp.dot(hn, wv_ref[...], preferred_element_type=F32)
    gh = jnp.dot(hh_scr[...].astype(BF16), wg_ref[...], preferred_element_type=F32)
    gh = jnp.where(i % seq_tiles == 0, 0.0, gh)
    row = lax.broadcasted_iota(jnp.int32, gate.shape, 0)
    roll1 = pltpu.roll(gate, 1, axis=0)
    roll2 = pltpu.roll(gate, 2, axis=0)
    pos = row % seq_new
    prev1 = jnp.where(is_sample,
                      jnp.where(pos == 0, e1_ref[...], roll1),
                      jnp.where(row == 0, gh[7:8], roll1))
    prev2 = jnp.where(is_sample,
                      jnp.where(pos < 2, e2_ref[...], roll2),
                      jnp.where(row == 0, gh[6:7], jnp.where(row == 1, gh[7:8], roll2)))
    conv = cb_ref[...] + prev2 * cw_ref[0:1] + prev1 * cw_ref[1:2] + gate * cw_ref[2:3]
    act = (conv * jax.nn.sigmoid(conv) * val).astype(BF16)
    act_scr[...] = act

    @pl.when(j == pl.num_programs(1) - 1)
    def _():
        y = x_tile() + acc_scr[...] + jnp.dot(act_scr[...], wd_ref[...], preferred_element_type=F32)
        if final_norm:
            y = y * lax.rsqrt(jnp.mean(y * y, axis=-1, keepdims=True) + NORM_EPS) * gf_ref[...]

        @pl.when(is_sample)
        def _():
            os_ref[...] = y

        @pl.when(jnp.logical_not(is_sample))
        def _():
            op_ref[...] = y


def conv_glu_ffn(xp, xs, g, w_up, conv_w, conv_b, w_down, e1, e2, g_final, seq_len, seq_new, tm=512, tf=256):
    Mp, D = xp.shape
    F = w_down.shape[0]
    assert Mp % tm == 0 and F % tf == 0 and seq_len % tm == 0 and tm % seq_new == 0
    assert xs.shape == (tm, D) and e1.shape == (tm, F) and e2.shape == (tm, F)
    nf = F // tf
    sample_tile = Mp // tm
    prow = pl.BlockSpec((tm, D), lambda i, j: (jnp.minimum(i, sample_tile - 1), 0))
    srow = pl.BlockSpec((tm, D), lambda i, j: (0, 0))
    halo = pl.BlockSpec((SUBLANES, D),
                        lambda i, j: (jnp.clip(i * (tm // SUBLANES) - 1, 0, Mp // SUBLANES - 1), 0))
    vec = pl.BlockSpec((1, D), lambda i, j: (0, 0))
    fill = pl.BlockSpec((tm, tf), lambda i, j: (0, jnp.where(i == sample_tile, j, 0)))
    w_up = w_up.astype(BF16)
    w_down = w_down.astype(BF16)
    return pl.pallas_call(
        functools.partial(_ffn_kernel, seq_tiles=seq_len // tm, sample_tile=sample_tile, seq_new=seq_new,
                          final_norm=g_final is not None),
        grid=(sample_tile + 1, nf),
        in_specs=[prow, srow, halo, vec,
                  pl.BlockSpec((D, tf), lambda i, j: (0, j)),
                  pl.BlockSpec((D, tf), lambda i, j: (0, j + nf)),
                  pl.BlockSpec((CONV_W, tf), lambda i, j: (0, j)),
                  pl.BlockSpec((1, tf), lambda i, j: (0, j)),
                  pl.BlockSpec((tf, D), lambda i, j: (j, 0)),
                  pl.BlockSpec((tf, D), lambda i, j: (jnp.maximum(j - 1, 0), 0)),
                  fill, fill, vec],
        out_specs=[prow, srow],
        out_shape=[jax.ShapeDtypeStruct((Mp, D), F32), jax.ShapeDtypeStruct((tm, D), F32)],
        scratch_shapes=[pltpu.VMEM((tm, D), BF16), pltpu.VMEM((SUBLANES, D), F32), pltpu.VMEM((tm, D), F32),
                        pltpu.VMEM((tm, tf), BF16)],
        compiler_params=_params("arbitrary", "arbitrary"),
        name="conv_glu_ffn",
    )(xp, xs, xp, g.reshape(1, D), w_up, w_up, conv_w, conv_b.reshape(1, F), w_down, w_down, e1, e2,
      (g if g_final is None else g_final).reshape(1, D))


def _rmsnorm(x, g):
    return x * lax.rsqrt(jnp.mean(x * x, axis=-1, keepdims=True) + NORM_EPS) * g


def _rope_angles(pos):
    half = HEAD_DIM // 2
    inv = jnp.exp(-math.log(ROPE_THETA) * jnp.arange(half, dtype=F32) * 2.0 / HEAD_DIM)
    return pos.astype(F32)[:, None] * inv[None, :]


def _rope_tables(pos):
    ang = _rope_angles(pos)
    cos, sin = jnp.cos(ang), jnp.sin(ang)
    return jnp.tile(cos, (1, LANES // cos.shape[1])), jnp.tile(jnp.concatenate([-sin, sin], axis=1), (1, 2))


def _rope(x, pos):
    half = HEAD_DIM // 2
    ang = _rope_angles(pos)
    cos = jnp.cos(ang)[:, None, None, :]
    sin = jnp.sin(ang)[:, None, None, :]
    x1, x2 = x[..., :half], x[..., half:]
    return jnp.concatenate([x1 * cos - x2 * sin, x2 * cos + x1 * sin], axis=-1)


def kernel(x_prompt, x_sample, state_rwkv_shift, state_rwkv_wkv, cache_k_w128, cache_v_w128, cache_k_w512, cache_v_w512, cache_k_w2048, cache_v_w2048, state_ffn_conv, norm_mix, norm_ffn, norm_final, rwkv_mu, rwkv_w_rkv, rwkv_w0, rwkv_w1, rwkv_w2, rwkv_a0, rwkv_a1, rwkv_a2, rwkv_g1, rwkv_g2, rwkv_k_k, rwkv_k_a, rwkv_r_k, rwkv_ln_w, rwkv_ln_b, rwkv_w_o, attn_w_in, attn_w_o, ffn_w_up, ffn_conv_w, ffn_conv_b, ffn_w_down):
    Bp, Tp, D = x_prompt.shape
    Bs, Ts, _ = x_sample.shape
    Mp, Ms = Bp * Tp, Bs * Ts
    assert Ts >= CONV_W - 1 and Tp >= CONV_W - 1 and CONV_W == 3
    xp = x_prompt.reshape(Mp, D)
    xs = x_sample.reshape(Ms, D)
    caches_k = (cache_k_w128, cache_k_w512, cache_k_w2048)
    caches_v = (cache_v_w128, cache_v_w512, cache_v_w2048)

    p_shift, p_wkv, s_shift, s_wkv = [], [], [], []
    p_k = [[] for _ in range(N_GROUPS)]
    p_v = [[] for _ in range(N_GROUPS)]
    s_k = [[] for _ in range(N_GROUPS)]
    s_v = [[] for _ in range(N_GROUPS)]
    p_conv, s_conv = [], []

    def time_mix(i, li, x, carried, s0, B, T):
        vec = jnp.stack([rwkv_w0[li], rwkv_a0[li], rwkv_k_k[li], rwkv_k_a[li], rwkv_r_k[li].reshape(D)])
        outs = rwkv_prep(
            x, x if carried is None else carried, norm_mix[i], rwkv_mu[li], rwkv_w_rkv[li], rwkv_w1[li], rwkv_w2[li],
            rwkv_a1[li], rwkv_a2[li], rwkv_g1[li], rwkv_g2[li], vec, T, 0 if carried is None else T)
        if carried is None:
            v, gate, sc = outs[7:]
            y, st = wkv_scan_tokens(outs[:6], outs[6])
            y, st = y.reshape(B * T, D), _state_out(st, B)
        else:
            a, wr, w, b, k, v, gate, sc = outs
            y, st = _wkv_carried(a, wr, w, b, k, v, sc[:, :N_HEADS], sc[:, N_HEADS:2 * N_HEADS], s0, B, T)
        return rwkv_post(y, v, gate, sc, x, rwkv_w_o[li], rwkv_ln_w[li], rwkv_ln_b[li]), st

    depth = norm_mix.shape[0]
    for i in range(depth):
        li = i // 2
        if i % 2 == 0:
            p_shift.append(_rmsnorm(xp.reshape(Bp, Tp, D)[:, -1], norm_mix[i]))
            s_shift.append(_rmsnorm(xs.reshape(Bs, Ts, D)[:, -1], norm_mix[i]))
            xp, stp = time_mix(i, li, xp, None, None, Bp, Tp)
            xs, sts = time_mix(i, li, xs, jnp.repeat(state_rwkv_shift[li], Ts, axis=0), state_rwkv_wkv[li], Bs, Ts)
            p_wkv.append(stp)
            s_wkv.append(sts)
        else:
            hs = _rmsnorm(xs, norm_mix[i])
            assert len(caches_k[0].shape) == 5 and caches_k[0].shape[0] == 1 and li == 0
            hgs = norm_classes(xp, norm_mix[i], DILATIONS, Tp)
            outs, lses = [], []
            for g in range(N_GROUPS):
                dil = DILATIONS[g]
                L = Tp // dil
                keep = min(WINDOWS[g], Tp)
                assert Tp % (dil * BAND) == 0 and keep % dil == 0
                qkv_g = mm(hgs[g], attn_w_in[li][:, g * 3 * D:(g + 1) * 3 * D]).reshape(Bp * dil, L, 3 * D)
                pos = (jnp.arange(dil)[:, None] + dil * jnp.arange(L)[None, :]).reshape(-1)
                o, lse, kr = band_attn(qkv_g, *_rope_tables(pos), dil)
                outs.append(o)
                lses.append(lse)
                tail = lambda t: t.reshape(Bp, dil, L, D)[:, :, L - keep // dil:].transpose(0, 2, 1, 3).reshape(
                    Bp, keep, N_HEADS, HEAD_DIM)
                p_k[g].append(tail(kr))
                p_v[g].append(tail(qkv_g[..., 2 * D:]))
            qkv_s = mm(hs.astype(BF16), attn_w_in[li]).reshape(Bs, Ts, N_GROUPS, 3, N_HEADS, HEAD_DIM)
            pos_s = PAST_LEN + jnp.arange(Ts)
            q_s = _rope(qkv_s[:, :, :, 0], pos_s)
            k_s = _rope(qkv_s[:, :, :, 1], pos_s)
            v_s = qkv_s[:, :, :, 2]
            rows8 = lambda t: jnp.pad(t.transpose(0, 2, 3, 1, 4), ((0, 0),) * 3 + ((0, SUBLANES - Ts), (0, 0)))
            osmp = sample_attn(rows8(q_s), rows8(k_s), rows8(v_s), caches_k, caches_v, Ts)
            osmp = osmp[:, :, :Ts].transpose(0, 2, 1, 3).reshape(Ms, D)
            for g in range(N_GROUPS):
                s_k[g].append(k_s[:, :, g])
                s_v[g].append(v_s[:, :, g])
            xp = attn_merge(outs, lses, DILATIONS, xp, attn_w_o[li], Tp)
            xs = xs + mm(osmp, attn_w_o[li])

        buf = state_ffn_conv[i]
        first = (jnp.arange(Ms) % Ts == 0)[:, None]
        e1 = jnp.repeat(buf[:, 1], Ts, axis=0)
        e2 = jnp.where(first, jnp.repeat(buf[:, 0], Ts, axis=0), e1)
        tail = jnp.concatenate([xp.reshape(Bp, Tp, D)[:, Tp - 2:].reshape(Bp * 2, D),
                                xs.reshape(Bs, Ts, D)[:, Ts - 2:].reshape(Bs * 2, D)], axis=0)
        gate_tail = mm(_rmsnorm(tail, norm_ffn[i]), ffn_w_up[i][:, :D_FF])
        p_conv.append(gate_tail[:Bp * 2].reshape(Bp, 2, D_FF))
        s_conv.append(gate_tail[Bp * 2:].reshape(Bs, 2, D_FF))
        xp, xs = conv_glu_ffn(xp, xs, norm_ffn[i], ffn_w_up[i], ffn_conv_w[i], ffn_conv_b[i], ffn_w_down[i],
                              e1, e2, norm_final if i == depth - 1 else None, Tp, Ts)

    y_prompt = xp.reshape(Bp, Tp, D)
    y_sample = xs.reshape(Bs, Ts, D)
    st = jnp.stack
    return (y_prompt, y_sample,
            st(p_shift), st(p_wkv),
            st(p_k[0]), st(p_v[0]), st(p_k[1]), st(p_v[1]), st(p_k[2]), st(p_v[2]), st(p_conv),
            st(s_shift), st(s_wkv),
            st(s_k[0]), st(s_v[0]), st(s_k[1]), st(s_v[1]), st(s_k[2]), st(s_v[2]), st(s_conv))
```

```python
import functools
import math

import jax
import jax.numpy as jnp
from jax import lax
from jax.experimental import pallas as pl
from jax.experimental.pallas import tpu as pltpu

F32 = jnp.float32
BF16 = jnp.bfloat16

D_MODEL = 1024
HEAD_DIM = 64
N_HEADS = 16
D_FF = 2816
CONV_W = 3
N_GROUPS = 3
WINDOWS = (128, 512, 2048)
DILATIONS = (1, 4, 16)
BAND = 128
GN_EPS = 64e-5
NORM_EPS = 1e-6
ROPE_THETA = 10000.0
PAST_LEN = 2048
LANES = 128
SUBLANES = 8
VMEM_LIMIT = 56 * 1024 * 1024
SCAN_VMEM_LIMIT = 62 * 1024 * 1024


def _params(*sem):
    return pltpu.CompilerParams(dimension_semantics=sem, vmem_limit_bytes=VMEM_LIMIT)


def _mm_kernel(x_ref, w_ref, o_ref):
    o_ref[...] = jnp.dot(x_ref[...].astype(BF16), w_ref[...],
                         preferred_element_type=F32).astype(o_ref.dtype)


def mm(x, w, tm=2048, tn=1024):
    M, K = x.shape
    N = w.shape[1]
    tm = next(t for t in (tm, tm // 2, M) if M % t == 0)
    tn = next(t for t in (tn, tn // 2, N) if N % t == 0)
    return pl.pallas_call(
        _mm_kernel,
        grid=(M // tm, N // tn),
        in_specs=[pl.BlockSpec((tm, K), lambda i, j: (i, 0)),
                  pl.BlockSpec((K, tn), lambda i, j: (0, j))],
        out_specs=pl.BlockSpec((tm, tn), lambda i, j: (i, j)),
        out_shape=jax.ShapeDtypeStruct((M, N), F32),
        compiler_params=_params("parallel", "parallel"),
        name="mm",
    )(x, w.astype(BF16))


VALUE_ROWS = 4


def _wkv_state_kernel(a_ref, wr_ref, w_ref, b_ref, k_ref, v_ref, br_ref, kr_ref, s_ref, y_ref, so_ref, *, steps):
    def rows(c, carry):
        v0 = pl.multiple_of(c * VALUE_ROWS, VALUE_ROWS)
        s = s_ref[0, pl.ds(v0, VALUE_ROWS)]
        for t in range(steps):
            sa = jnp.sum(s * a_ref[t, 0][None], axis=1, keepdims=True)
            yy = jnp.sum(s * wr_ref[t, 0][None], axis=1, keepdims=True)
            v = v_ref[t, 0, pl.ds(v0, VALUE_ROWS), :][:, None, :]
            y = yy + sa * br_ref[t, 0][None] + v * kr_ref[t, 0][None]
            y_ref[t, 0, pl.ds(v0, VALUE_ROWS), :] = y[:, 0, :]
            s = s * w_ref[t, 0][None] + sa * b_ref[t, 0][None] + v * k_ref[t, 0][None]
        so_ref[0, pl.ds(v0, VALUE_ROWS)] = s
        return carry

    lax.fori_loop(0, HEAD_DIM // VALUE_ROWS, rows, 0)


def wkv_scan_state(ops, br, kr, s0):
    T, H, dh, B = ops[0].shape
    op_spec = pl.BlockSpec((T, 1, dh, B), lambda h: (0, h, 0, 0))
    sc_spec = pl.BlockSpec((T, 1, 1, B), lambda h: (0, h, 0, 0))
    st_spec = pl.BlockSpec((1, dh, dh, B), lambda h: (h, 0, 0, 0))
    return pl.pallas_call(
        functools.partial(_wkv_state_kernel, steps=T),
        grid=(H,),
        in_specs=[op_spec] * 6 + [sc_spec, sc_spec, st_spec],
        out_specs=[op_spec, st_spec],
        out_shape=[jax.ShapeDtypeStruct((T, H, dh, B), F32), jax.ShapeDtypeStruct(s0.shape, F32)],
        compiler_params=_params("parallel"),
        name="wkv_scan_state",
    )(*ops, br, kr, s0)


CHUNK = LANES
KEYS_PER_TRIP = 32


def _wkv_chunk_kernel(a_ref, wr_ref, w_ref, b_ref, k_ref, v_ref, sc_ref, y_ref, st_ref,
                      state, zrow, zv, zs, zy, xt):
    c = pl.program_id(0)
    nb = a_ref.shape[0]
    rows_bh = nb * N_HEADS

    @pl.when(c == 0)
    def _():
        state[...] = jnp.zeros_like(state)

    def lanes_tile(ref, first, second, stride):
        top = ref[:, pl.ds(first, N_HEADS, stride=stride), :].reshape(rows_bh, CHUNK)
        bot = ref[:, pl.ds(second, N_HEADS, stride=stride), :].reshape(rows_bh, CHUNK)
        return jnp.concatenate([top, bot], axis=0).T

    def grouped(n, body):
        def trip(g, carry):
            for u in range(SUBLANES):
                body(g * SUBLANES + u)
            return carry
        lax.fori_loop(0, n // SUBLANES, trip, 0)

    for op, ref in enumerate((a_ref, wr_ref, w_ref, b_ref, k_ref)):
        def fill(k, op=op, ref=ref):
            zrow[op, k] = lanes_tile(ref, k, k, HEAD_DIM)
        grouped(HEAD_DIM, fill)

    def fill_v(v_lo):
        zv[pl.ds(pl.multiple_of(v_lo * CHUNK, CHUNK), CHUNK), :] = lanes_tile(
            v_ref, v_lo, HEAD_DIM // 2 + v_lo, HEAD_DIM)
    grouped(HEAD_DIM // 2, fill_v)

    zs[0] = lanes_tile(sc_ref, 0, 0, 1)
    zs[1] = lanes_tile(sc_ref, N_HEADS, N_HEADS, 1)

    def row(op, k, t):
        return jnp.broadcast_to(zrow[op, k, pl.ds(t, 1), :], (SUBLANES, LANES))[None]

    def step(t, carry):
        def reduce_keys(g, acc):
            sa, yy = acc
            for u in range(KEYS_PER_TRIP):
                k = g * KEYS_PER_TRIP + u
                sk = state[k]
                sa = sa + sk * row(0, k, t)
                yy = yy + sk * row(1, k, t)
            return sa, yy
        zero = jnp.zeros((4, SUBLANES, LANES), F32)
        sa, yy = lax.fori_loop(0, HEAD_DIM // KEYS_PER_TRIP, reduce_keys, (zero, zero))
        v = zv[pl.ds(t, HEAD_DIM // 2, stride=CHUNK), :].reshape(4, SUBLANES, LANES)
        br = jnp.broadcast_to(zs[0, pl.ds(t, 1), :], (SUBLANES, LANES))[None]
        kr = jnp.broadcast_to(zs[1, pl.ds(t, 1), :], (SUBLANES, LANES))[None]
        zy[pl.ds(t, HEAD_DIM // 2, stride=CHUNK), :] = (yy + sa * br + v * kr).reshape(HEAD_DIM // 2, LANES)

        def update_keys(g, carry):
            for u in range(KEYS_PER_TRIP):
                k = g * KEYS_PER_TRIP + u
                state[k] = state[k] * row(2, k, t) + sa * row(3, k, t) + v * row(4, k, t)
            return carry
        lax.fori_loop(0, HEAD_DIM // KEYS_PER_TRIP, update_keys, 0)
        return carry

    lax.fori_loop(0, CHUNK, step, 0)

    def drain(v_lo):
        tile = zy[pl.ds(pl.multiple_of(v_lo * CHUNK, CHUNK), CHUNK), :].T
        for v_hi in range(2):
            xt[:, pl.ds(v_hi * (HEAD_DIM // 2) + v_lo, N_HEADS, stride=HEAD_DIM), :] = (
                tile[v_hi * rows_bh:(v_hi + 1) * rows_bh].reshape(nb, N_HEADS, CHUNK))
    grouped(HEAD_DIM // 2, drain)

    def to_token_major(b, carry):
        for cb in range(y_ref.shape[2] // LANES):
            y_ref[b, :, cb * LANES:(cb + 1) * LANES] = xt[b, cb * LANES:(cb + 1) * LANES, :].T
        return carry
    lax.fori_loop(0, nb, to_token_major, 0)

    @pl.when(c == pl.num_programs(0) - 1)
    def _():
        st_ref[...] = state[...]


def wkv_scan_tokens(ops, sc):
    B, D, T = ops[0].shape
    assert B * N_HEADS * 2 == LANES and T % CHUNK == 0 and D == N_HEADS * HEAD_DIM
    spec = pl.BlockSpec((B, D, CHUNK), lambda c: (0, 0, c))
    st_shape = (HEAD_DIM, 4, SUBLANES, LANES)
    return pl.pallas_call(
        _wkv_chunk_kernel,
        grid=(T // CHUNK,),
        in_specs=[spec] * 6 + [pl.BlockSpec((B, LANES, CHUNK), lambda c: (0, 0, c))],
        out_specs=[pl.BlockSpec((B, CHUNK, D), lambda c: (0, c, 0)), pl.BlockSpec(st_shape, lambda c: (0, 0, 0, 0))],
        out_shape=[jax.ShapeDtypeStruct((B, T, D), F32), jax.ShapeDtypeStruct(st_shape, F32)],
        scratch_shapes=[pltpu.VMEM(st_shape, F32),
                        pltpu.VMEM((5, HEAD_DIM, CHUNK, LANES), F32),
                        pltpu.VMEM((HEAD_DIM // 2 * CHUNK, LANES), F32),
                        pltpu.VMEM((2, CHUNK, LANES), F32),
                        pltpu.VMEM((HEAD_DIM // 2 * CHUNK, LANES), F32),
                        pltpu.VMEM((B, D, CHUNK), F32)],
        compiler_params=pltpu.CompilerParams(dimension_semantics=("arbitrary",), vmem_limit_bytes=SCAN_VMEM_LIMIT),
        name="wkv_scan_tokens",
    )(*ops, sc)


def _state_out(s, B):
    nl = B * N_HEADS // 64
    s = s.reshape(HEAD_DIM, 32, nl, 2, 64).transpose(2, 4, 3, 1, 0)
    return s.reshape(B, N_HEADS, HEAD_DIM, HEAD_DIM)


def _wkv_carried(a, wr, w, b, k, v, br, kr, s0, B, T):
    lanes_b = lambda t, n: t.reshape(B, T, N_HEADS, n).transpose(1, 2, 3, 0)
    y, st = wkv_scan_state([lanes_b(t, HEAD_DIM) for t in (a, wr, w, b, k, v)], lanes_b(br, 1), lanes_b(kr, 1),
                           jnp.transpose(s0, (1, 2, 3, 0)))
    return y.transpose(3, 0, 1, 2).reshape(B * T, D_MODEL), jnp.transpose(st, (3, 0, 1, 2))


def _dot_split(x, m):
    hi = x.astype(BF16)
    lo = (x - hi.astype(F32)).astype(BF16)
    return jnp.dot(hi, m, preferred_element_type=F32) + jnp.dot(lo, m, preferred_element_type=F32)


def _head_matrices():
    lane_head = jnp.arange(D_MODEL)[:, None] // HEAD_DIM
    col = jnp.arange(LANES)[None, :]
    seg = jnp.stack([(col == lane_head + N_HEADS * j) for j in range(3)]).astype(BF16)
    return seg, jnp.swapaxes(seg, 1, 2)


def _rwkv_prep_kernel(x_ref, halo_ref, e_ref, g_ref, mu_ref, wrkv_ref, w1_ref, w2_ref, a1_ref, a2_ref, g1_ref,
                      g2_ref, vec_ref, seg_ref, spread_ref, *outs, seq_tiles, seq_new):
    i = pl.program_id(0)

    def norm(x):
        return x * lax.rsqrt(jnp.mean(x * x, axis=-1, keepdims=True) + NORM_EPS) * g_ref[...]

    h = norm(x_ref[...])
    row = lax.broadcasted_iota(jnp.int32, h.shape, 0)
    shifted = pltpu.roll(h, 1, axis=0)
    if seq_new:
        prev = jnp.where(row % seq_new == 0, e_ref[...], shifted)
    else:
        before = jnp.where(i % seq_tiles == 0, 0.0, norm(halo_ref[...])[SUBLANES - 1:])
        prev = jnp.where(row == 0, before, shifted)
    xx = prev - h
    mix = lambda n: (h + xx * mu_ref[n:n + 1]).astype(BF16)
    proj = lambda x, w: jnp.dot(x, w, preferred_element_type=F32)
    w0, a0, k_k, k_a, r_k = (vec_ref[n:n + 1] for n in range(5))
    r = proj(mix(0), wrkv_ref[0])
    k = proj(mix(1), wrkv_ref[1])
    v = proj(mix(2), wrkv_ref[2])
    z = -(w0 + proj(jnp.tanh(proj(mix(3), w1_ref[...])).astype(BF16), w2_ref[...]))
    softplus = jnp.maximum(z, 0.0) + jnp.log(1.0 + jnp.exp(-jnp.abs(z)))
    w = jnp.exp(-jnp.exp(-softplus - 0.5))
    a = jax.nn.sigmoid(a0 + proj(proj(mix(4), a1_ref[...]).astype(BF16), a2_ref[...]))
    gate = proj(jax.nn.sigmoid(proj(mix(5), g1_ref[...])).astype(BF16), g2_ref[...])
    kk = k * k_k
    norm2 = _dot_split(_dot_split(kk * kk, seg_ref[0]), spread_ref[0])
    kk = kk / jnp.maximum(jnp.sqrt(norm2), 1e-12)
    k = k * (1.0 + (a - 1.0) * k_a)
    b = kk * a
    sc = _dot_split(b * r, seg_ref[0]) + _dot_split(k * r, seg_ref[1]) + _dot_split(r * k * r_k, seg_ref[2])
    scan_ops = (-kk, w * r, w, b, k, v)
    if seq_new:
        for ref, val in zip(outs[:6], scan_ops):
            ref[...] = val
        gate_out, sc_out = outs[6:]
    else:
        for ref, val in zip(outs[:6], scan_ops):
            ref[0] = val.T
        outs[6][0] = sc.T
        v_out, gate_out, sc_out = outs[7:]
        v_out[...] = v
    gate_out[...] = gate
    sc_out[...] = sc


def rwkv_prep(x, e, g, mu, w_rkv, w1, w2, a1, a2, g1, g2, vec, seq_len, seq_new, tm=256):
    M, D = x.shape
    assert M % tm == 0 and (tm % seq_new == 0 if seq_new else seq_len % tm == 0)
    seg, spread = _head_matrices()
    once = pl.Buffered(1)
    row = pl.BlockSpec((tm, D), lambda i: (i, 0))
    halo = pl.BlockSpec((SUBLANES, D), lambda i: (jnp.maximum(i * (tm // SUBLANES) - 1, 0), 0))
    full = lambda t: pl.BlockSpec(t.shape, lambda i: (0,) * t.ndim, pipeline_mode=once)
    consts = [g.reshape(1, D), mu] + [t.astype(BF16) for t in (w_rkv, w1, w2, a1, a2, g1, g2)] + [vec, seg, spread]
    out = jax.ShapeDtypeStruct((M, D), F32)
    packed = pl.BlockSpec((tm, LANES), lambda i: (i, 0))
    packed_out = jax.ShapeDtypeStruct((M, LANES), F32)
    if seq_new:
        out_specs = [row] * 7 + [packed]
        out_shape = [out] * 7 + [packed_out]
    else:
        per_seq = seq_len // tm
        cm = lambda n: pl.BlockSpec((1, n, tm), lambda i: (i // per_seq, 0, i % per_seq))
        cm_out = lambda n: jax.ShapeDtypeStruct((M // seq_len, n, seq_len), F32)
        out_specs = [cm(D)] * 6 + [cm(LANES), row, row, packed]
        out_shape = [cm_out(D)] * 6 + [cm_out(LANES), out, out, packed_out]
    return pl.pallas_call(
        functools.partial(_rwkv_prep_kernel, seq_tiles=max(seq_len // tm, 1), seq_new=seq_new),
        grid=(M // tm,),
        in_specs=[row, halo, row if seq_new else pl.BlockSpec((tm, D), lambda i: (0, 0))] + [full(t) for t in consts],
        out_specs=out_specs,
        out_shape=out_shape,
        compiler_params=_params("parallel"),
        name="rwkv_prep",
    )(x, x, e, *consts)


def _rwkv_post_kernel(y_ref, v_ref, gate_ref, sc_ref, x_ref, wo_ref, lnw_ref, lnb_ref, seg_ref, spread_ref, o_ref):
    y = y_ref[...]
    inv = 1.0 / HEAD_DIM
    mean = _dot_split(_dot_split(y, seg_ref[0]), spread_ref[0]) * inv
    d = y - mean
    var = _dot_split(_dot_split(d * d, seg_ref[0]), spread_ref[0]) * inv
    yn = d * lax.rsqrt(var + GN_EPS) * lnw_ref[...] + lnb_ref[...]
    bonus = _dot_split(sc_ref[...], spread_ref[2]) * v_ref[...]
    out = ((yn + bonus) * gate_ref[...]).astype(BF16)
    o_ref[...] = x_ref[...] + jnp.dot(out, wo_ref[...], preferred_element_type=F32)


def rwkv_post(y, v, gate, sc, x, w_o, ln_w, ln_b, tm=512):
    M, D = x.shape
    assert M % tm == 0
    seg, spread = _head_matrices()
    once = pl.Buffered(1)
    row = pl.BlockSpec((tm, D), lambda i: (i, 0))
    full = lambda t: pl.BlockSpec(t.shape, lambda i: (0,) * t.ndim, pipeline_mode=once)
    consts = [w_o.astype(BF16), ln_w.reshape(1, D), ln_b.reshape(1, D), seg, spread]
    return pl.pallas_call(
        _rwkv_post_kernel,
        grid=(M // tm,),
        in_specs=[row, row, row, pl.BlockSpec((tm, LANES), lambda i: (i, 0)), row] + [full(t) for t in consts],
        out_specs=row,
        out_shape=jax.ShapeDtypeStruct((M, D), F32),
        compiler_params=_params("parallel"),
        name="rwkv_post",
    )(y, v, gate, sc, x, *consts)


def _band_attn_kernel(q_ref, k_ref, v_ref, cos_ref, sin_ref, o_ref, lse_ref, kr_ref, kprev, vprev):
    n = pl.program_id(1)

    @pl.when(n == 0)
    def _():
        kprev[...] = jnp.zeros_like(kprev)
        vprev[...] = jnp.zeros_like(vprev)

    i = lax.broadcasted_iota(jnp.int32, (BAND, 2 * BAND), 0)
    j = lax.broadcasted_iota(jnp.int32, (BAND, 2 * BAND), 1)
    valid = (j >= i) & (j <= i + BAND) & ((n > 0) | (j >= BAND))
    lane = lax.broadcasted_iota(jnp.int32, (BAND, LANES), 1)
    first_head = lane < HEAD_DIM
    first_half = lane % HEAD_DIM < HEAD_DIM // 2
    cos, sin = cos_ref[...], sin_ref[...]

    def rope(x):
        partner = jnp.where(first_half, pltpu.roll(x, LANES - HEAD_DIM // 2, axis=1),
                            pltpu.roll(x, HEAD_DIM // 2, axis=1))
        return x * cos + partner * sin

    k_before, v_before = kprev[...], vprev[...]
    key_head = lax.broadcasted_iota(jnp.int32, (2 * BAND, LANES), 1) < HEAD_DIM
    k_now, v_now = [], []
    for p in range(N_HEADS // 2):
        cs = slice(p * LANES, (p + 1) * LANES)
        q = rope(q_ref[0, :, cs])
        k = rope(k_ref[0, :, cs])
        kr_ref[0, :, cs] = k
        k_now.append(k.astype(BF16))
        v_now.append(v_ref[0, :, cs].astype(BF16))
        kcat = jnp.concatenate([k_before[:, cs], k_now[-1]], axis=0)
        vcat = jnp.concatenate([v_before[:, cs], v_now[-1]], axis=0)
        tops, sums = [], []
        for head_mask, own in ((first_head, key_head), (jnp.logical_not(first_head), jnp.logical_not(key_head))):
            qm = jnp.where(head_mask, q, 0.0).astype(BF16)
            s = lax.dot_general(qm, kcat, (((1,), (1,)), ((), ())), preferred_element_type=F32)
            s = jnp.where(valid, s * (HEAD_DIM ** -0.5), -jnp.inf)
            m = jnp.max(s, axis=-1, keepdims=True)
            e = jnp.exp(s - m).astype(BF16)
            sums.append(jnp.dot(e, jnp.where(own, vcat, 1.0), preferred_element_type=F32))
            tops.append(m)
        den = pltpu.roll(jnp.where(first_head, sums[1], sums[0]), HEAD_DIM, axis=1)
        o_ref[0, :, cs] = jnp.where(first_head, sums[0], sums[1]) / den
        lse_ref[0, :, cs] = jnp.where(first_head, tops[0], tops[1]) + jnp.log(den)
    kprev[...] = jnp.concatenate(k_now, axis=1)
    vprev[...] = jnp.concatenate(v_now, axis=1)


def band_attn(qkv, cos, sin, dil):
    S, L, W = qkv.shape
    D = W // 3
    nb = L // BAND
    col = lambda m: pl.BlockSpec((1, BAND, D), lambda s, n: (s, n, m))
    tab = pl.BlockSpec((BAND, LANES), lambda s, n: ((s % dil) * nb + n, 0))
    out = jax.ShapeDtypeStruct((S, L, D), F32)
    return pl.pallas_call(
        _band_attn_kernel,
        grid=(S, nb),
        in_specs=[col(0), col(1), col(2), tab, tab],
        out_specs=[col(0)] * 3,
        out_shape=[out] * 3,
        scratch_shapes=[pltpu.VMEM((BAND, D), BF16), pltpu.VMEM((BAND, D), BF16)],
        compiler_params=_params("parallel", "arbitrary"),
        name="band_attn",
    )(qkv, qkv, qkv, cos, sin)


def _norm_classes_kernel(x_ref, g_ref, *refs, dils):
    outs, scr = refs[:len(dils)], refs[len(dils)]
    x = x_ref[...]
    h = x * lax.rsqrt(jnp.mean(x * x, axis=-1, keepdims=True) + NORM_EPS) * g_ref[...]
    for j in range(scr.shape[0]):
        scr[j] = h[:, j * LANES:(j + 1) * LANES]
    for ref, dil in zip(outs, dils):
        if dil == 1:
            ref[0, 0] = h.astype(BF16)
            continue
        for c in range(dil):
            for j in range(scr.shape[0]):
                ref[0, c, :, j * LANES:(j + 1) * LANES] = scr[j, pl.ds(c, ref.shape[2], stride=dil), :].astype(BF16)


def norm_classes(x, g, dils, seq_len, tm=256):
    M, D = x.shape
    assert seq_len % tm == 0 and all(tm % (d * 2 * SUBLANES) == 0 for d in dils)
    per_seq = seq_len // tm
    spec = lambda d: pl.BlockSpec((1, d, tm // d, D), lambda i: (i // per_seq, 0, i % per_seq, 0))
    shape = lambda d: jax.ShapeDtypeStruct((M // seq_len, d, seq_len // d, D), BF16)
    outs = pl.pallas_call(
        functools.partial(_norm_classes_kernel, dils=tuple(dils)),
        grid=(M // tm,),
        in_specs=[pl.BlockSpec((tm, D), lambda i: (i, 0)), pl.BlockSpec((1, D), lambda i: (0, 0))],
        out_specs=[spec(d) for d in dils],
        out_shape=[shape(d) for d in dils],
        scratch_shapes=[pltpu.VMEM((D // LANES, tm, LANES), F32)],
        compiler_params=_params("parallel"),
        name="norm_classes",
    )(x, g.reshape(1, D))
    return [o.reshape(M, D) for o in outs]


def _attn_merge_kernel(*refs, dils):
    n = len(dils)
    o_refs, l_refs = refs[:n], refs[n:2 * n]
    x_ref, wo_ref, out_ref, scr = refs[2 * n:]

    def token_major(ref, dil):
        if dil == 1:
            return ref[0, 0]
        for c in range(dil):
            for j in range(scr.shape[0]):
                scr[j, pl.ds(c, ref.shape[2], stride=dil), :] = ref[0, c, :, j * LANES:(j + 1) * LANES]
        return jnp.concatenate([scr[j] for j in range(scr.shape[0])], axis=1)

    outs = [token_major(r, d) for r, d in zip(o_refs, dils)]
    lses = [token_major(r, d) for r, d in zip(l_refs, dils)]
    top = functools.reduce(jnp.maximum, lses)
    wts = [jnp.exp(l - top) for l in lses]
    merged = sum(w * o for w, o in zip(wts, outs)) / sum(wts)
    out_ref[...] = x_ref[...] + jnp.dot(merged.astype(BF16), wo_ref[...], preferred_element_type=F32)


def attn_merge(outs, lses, dils, x, w_o, seq_len, tm=512):
    M, D = x.shape
    assert seq_len % tm == 0 and all(tm % (d * SUBLANES) == 0 for d in dils)
    per_seq = seq_len // tm
    group = lambda t, d: t.reshape(M // seq_len, d, seq_len // d, D)
    spec = lambda d: pl.BlockSpec((1, d, tm // d, D), lambda i: (i // per_seq, 0, i % per_seq, 0))
    row = pl.BlockSpec((tm, D), lambda i: (i, 0))
    return pl.pallas_call(
        functools.partial(_attn_merge_kernel, dils=tuple(dils)),
        grid=(M // tm,),
        in_specs=[spec(d) for d in dils] * 2 + [row, pl.BlockSpec((D, D), lambda i: (0, 0))],
        out_specs=row,
        out_shape=jax.ShapeDtypeStruct((M, D), F32),
        scratch_shapes=[pltpu.VMEM((D // LANES, tm, LANES), F32)],
        compiler_params=_params("parallel"),
        name="attn_merge",
    )(*[group(t, d) for t, d in zip(outs, dils)], *[group(t, d) for t, d in zip(lses, dils)], x, w_o.astype(BF16))


def _sample_attn_kernel(q_ref, kn_ref, vn_ref, ck0_ref, cv0_ref, ck1_ref, cv1_ref, ck2_ref, cv2_ref,
                        o_ref, *, n_new):
    scale = HEAD_DIM ** -0.5
    caches = ((ck0_ref, cv0_ref), (ck1_ref, cv1_ref), (ck2_ref, cv2_ref))
    jcol = lax.broadcasted_iota(jnp.int32, (SUBLANES, 1), 0)
    masks = []
    for ck_ref, _ in caches:
        rows = ck_ref.shape[-1]
        dil = rows // BAND
        qi = lax.broadcasted_iota(jnp.int32, (SUBLANES, rows), 0)
        ri = lax.broadcasted_iota(jnp.int32, (SUBLANES, rows), 1)
        masks.append(ri >= qi if dil == 1 else ri % dil == qi)

    def head(h, carry):
        sc, sn = [], []
        for g, (ck_ref, _) in enumerate(caches):
            q = q_ref[0, g, h]
            s = jnp.dot(q.astype(BF16), ck_ref[0, 0, h].astype(BF16), preferred_element_type=F32) * scale
            sc.append(jnp.where(masks[g], s, -jnp.inf))
            kn = kn_ref[0, g, h]
            for i in range(n_new):
                si = jnp.sum(q * kn[i:i + 1, :], axis=-1, keepdims=True) * scale
                sn.append(jnp.where(jcol >= i if g == 0 else jcol == i, si, -jnp.inf))
        m = functools.reduce(jnp.maximum, [jnp.max(s, axis=-1, keepdims=True) for s in sc] + sn)
        den = jnp.zeros((SUBLANES, 1), F32)
        acc = jnp.zeros((SUBLANES, HEAD_DIM), F32)
        for g, (_, cv_ref) in enumerate(caches):
            e = jnp.exp(sc[g] - m)
            den = den + jnp.sum(e, axis=-1, keepdims=True)
            acc = acc + lax.dot_general(e.astype(BF16), cv_ref[0, 0, h].astype(BF16),
                                        (((1,), (1,)), ((), ())), preferred_element_type=F32)
            vn = vn_ref[0, g, h]
            for i in range(n_new):
                en = jnp.exp(sn[g * n_new + i] - m)
                den = den + en
                acc = acc + en * vn[i:i + 1, :]
        o_ref[0, h] = acc / den
        return carry

    def head_pair(hh, carry):
        head(2 * hh, carry)
        return head(2 * hh + 1, carry)

    lax.fori_loop(0, N_HEADS // 2, head_pair, 0)


def sample_attn(q, kn, vn, caches_k, caches_v, n_new):
    B = q.shape[0]
    new_spec = pl.BlockSpec((1, N_GROUPS, N_HEADS, SUBLANES, HEAD_DIM), lambda b: (b, 0, 0, 0, 0))
    specs = [new_spec] * 3
    args = [q, kn, vn]
    for g in range(N_GROUPS):
        rows = caches_k[g].shape[2]
        assert rows == BAND * DILATIONS[g] and n_new <= SUBLANES and (g == 0 or n_new <= DILATIONS[g])
        spec = pl.BlockSpec((1, 1, N_HEADS, HEAD_DIM, rows), lambda b: (0, b, 0, 0, 0))
        specs += [spec, spec]
        args += [jnp.transpose(caches_k[g], (0, 1, 3, 4, 2)), jnp.transpose(caches_v[g], (0, 1, 3, 4, 2))]
    return pl.pallas_call(
        functools.partial(_sample_attn_kernel, n_new=n_new),
        grid=(B,),
        in_specs=specs,
        out_specs=pl.BlockSpec((1, N_HEADS, SUBLANES, HEAD_DIM), lambda b: (b, 0, 0, 0)),
        out_shape=jax.ShapeDtypeStruct((B, N_HEADS, SUBLANES, HEAD_DIM), F32),
        compiler_params=_params("parallel"),
        name="sample_attn",
    )(*args)


def _ffn_kernel(xp_ref, xs_ref, halo_ref, g_ref, wg_ref, wv_ref, cw_ref, cb_ref, wd_ref, wdl_ref, e1_ref, e2_ref,
                gf_ref, op_ref, os_ref, hn_scr, hh_scr, acc_scr, act_scr, *, seq_tiles, sample_tile, seq_new,
                final_norm):
    i = pl.program_id(0)
    j = pl.program_id(1)
    is_sample = i == sample_tile

    def norm(x):
        return (x * lax.rsqrt(jnp.mean(x * x, axis=-1, keepdims=True) + NORM_EPS) * g_ref[...]).astype(BF16)

    def x_tile():
        return jnp.where(is_sample, xs_ref[...], xp_ref[...])

    @pl.when(j == 0)
    def _():
        hn_scr[...] = norm(x_tile())
        hh_scr[...] = norm(halo_ref[...]).astype(F32)
        acc_scr[...] = jnp.zeros_like(acc_scr)
        act_scr[...] = jnp.zeros_like(act_scr)

    acc_scr[...] += jnp.dot(act_scr[...], wdl_ref[...], preferred_element_type=F32)

    hn = hn_scr[...]
    gate = jnp.dot(hn, wg_ref[...], preferred_element_type=F32)
    val = jnp.dot(hn, wv_ref[...], preferred_element_type=F32)
    gh = jnp.dot(hh_scr[...].astype(BF16), wg_ref[...], preferred_element_type=F32)
    gh = jnp.where(i % seq_tiles == 0, 0.0, gh)
    row = lax.broadcasted_iota(jnp.int32, gate.shape, 0)
    roll1 = pltpu.roll(gate, 1, axis=0)
    roll2 = pltpu.roll(gate, 2, axis=0)
    pos = row % seq_new
    prev1 = jnp.where(is_sample,
                      jnp.where(pos == 0, e1_ref[...], roll1),
                      jnp.where(row == 0, gh[7:8], roll1))
    prev2 = jnp.where(is_sample,
                      jnp.where(pos < 2, e2_ref[...], roll2),
                      jnp.where(row == 0, gh[6:7], jnp.where(row == 1, gh[7:8], roll2)))
    conv = cb_ref[...] + prev2 * cw_ref[0:1] + prev1 * cw_ref[1:2] + gate * cw_ref[2:3]
    act = (conv * jax.nn.sigmoid(conv) * val).astype(BF16)
    act_scr[...] = act

    @pl.when(j == pl.num_programs(1) - 1)
    def _():
        y = x_tile() + acc_scr[...] + jnp.dot(act_scr[...], wd_ref[...], preferred_element_type=F32)
        if final_norm:
            y = y * lax.rsqrt(jnp.mean(y * y, axis=-1, keepdims=True) + NORM_EPS) * gf_ref[...]

        @pl.when(is_sample)
        def _():
            os_ref[...] = y

        @pl.when(jnp.logical_not(is_sample))
        def _():
            op_ref[...] = y


def conv_glu_ffn(xp, xs, g, w_up, conv_w, conv_b, w_down, e1, e2, g_final, seq_len, seq_new, tm=512, tf=256):
    Mp, D = xp.shape
    F = w_down.shape[0]
    assert Mp % tm == 0 and F % tf == 0 and seq_len % tm == 0 and tm % seq_new == 0
    assert xs.shape == (tm, D) and e1.shape == (tm, F) and e2.shape == (tm, F)
    nf = F // tf
    sample_tile = Mp // tm
    prow = pl.BlockSpec((tm, D), lambda i, j: (jnp.minimum(i, sample_tile - 1), 0))
    srow = pl.BlockSpec((tm, D), lambda i, j: (0, 0))
    halo = pl.BlockSpec((SUBLANES, D),
                        lambda i, j: (jnp.clip(i * (tm // SUBLANES) - 1, 0, Mp // SUBLANES - 1), 0))
    vec = pl.BlockSpec((1, D), lambda i, j: (0, 0))
    fill = pl.BlockSpec((tm, tf), lambda i, j: (0, jnp.where(i == sample_tile, j, 0)))
    w_up = w_up.astype(BF16)
    w_down = w_down.astype(BF16)
    return pl.pallas_call(
        functools.partial(_ffn_kernel, seq_tiles=seq_len // tm, sample_tile=sample_tile, seq_new=seq_new,
                          final_norm=g_final is not None),
        grid=(sample_tile + 1, nf),
        in_specs=[prow, srow, halo, vec,
                  pl.BlockSpec((D, tf), lambda i, j: (0, j)),
                  pl.BlockSpec((D, tf), lambda i, j: (0, j + nf)),
                  pl.BlockSpec((CONV_W, tf), lambda i, j: (0, j)),
                  pl.BlockSpec((1, tf), lambda i, j: (0, j)),
                  pl.BlockSpec((tf, D), lambda i, j: (j, 0)),
                  pl.BlockSpec((tf, D), lambda i, j: (jnp.maximum(j - 1, 0), 0)),
                  fill, fill, vec],
        out_specs=[prow, srow],
        out_shape=[jax.ShapeDtypeStruct((Mp, D), F32), jax.ShapeDtypeStruct((tm, D), F32)],
        scratch_shapes=[pltpu.VMEM((tm, D), BF16), pltpu.VMEM((SUBLANES, D), F32), pltpu.VMEM((tm, D), F32),
                        pltpu.VMEM((tm, tf), BF16)],
        compiler_params=_params("arbitrary", "arbitrary"),
        name="conv_glu_ffn",
    )(xp, xs, xp, g.reshape(1, D), w_up, w_up, conv_w, conv_b.reshape(1, F), w_down, w_down, e1, e2,
      (g if g_final is None else g_final).reshape(1, D))


def _rmsnorm(x, g):
    return x * lax.rsqrt(jnp.mean(x * x, axis=-1, keepdims=True) + NORM_EPS) * g


def _rope_angles(pos):
    half = HEAD_DIM // 2
    inv = jnp.exp(-math.log(ROPE_THETA) * jnp.arange(half, dtype=F32) * 2.0 / HEAD_DIM)
    return pos.astype(F32)[:, None] * inv[None, :]


def _rope_tables(pos):
    ang = _rope_angles(pos)
    cos, sin = jnp.cos(ang), jnp.sin(ang)
    return jnp.tile(cos, (1, LANES // cos.shape[1])), jnp.tile(jnp.concatenate([-sin, sin], axis=1), (1, 2))


def _rope(x, pos):
    half = HEAD_DIM // 2
    ang = _rope_angles(pos)
    cos = jnp.cos(ang)[:, None, None, :]
    sin = jnp.sin(ang)[:, None, None, :]
    x1, x2 = x[..., :half], x[..., half:]
    return jnp.concatenate([x1 * cos - x2 * sin, x2 * cos + x1 * sin], axis=-1)


def kernel(x_prompt, x_sample, state_rwkv_shift, state_rwkv_wkv, cache_k_w128, cache_v_w128, cache_k_w512, cache_v_w512, cache_k_w2048, cache_v_w2048, state_ffn_conv, norm_mix, norm_ffn, norm_final, rwkv_mu, rwkv_w_rkv, rwkv_w0, rwkv_w1, rwkv_w2, rwkv_a0, rwkv_a1, rwkv_a2, rwkv_g1, rwkv_g2, rwkv_k_k, rwkv_k_a, rwkv_r_k, rwkv_ln_w, rwkv_ln_b, rwkv_w_o, attn_w_in, attn_w_o, ffn_w_up, ffn_conv_w, ffn_conv_b, ffn_w_down):
    Bp, Tp, D = x_prompt.shape
    Bs, Ts, _ = x_sample.shape
    Mp, Ms = Bp * Tp, Bs * Ts
    assert Ts >= CONV_W - 1 and Tp >= CONV_W - 1 and CONV_W == 3
    xp = x_prompt.reshape(Mp, D)
    xs = x_sample.reshape(Ms, D)
    caches_k = (cache_k_w128, cache_k_w512, cache_k_w2048)
    caches_v = (cache_v_w128, cache_v_w512, cache_v_w2048)

    p_shift, p_wkv, s_shift, s_wkv = [], [], [], []
    p_k = [[] for _ in range(N_GROUPS)]
    p_v = [[] for _ in range(N_GROUPS)]
    s_k = [[] for _ in range(N_GROUPS)]
    s_v = [[] for _ in range(N_GROUPS)]
    p_conv, s_conv = [], []

    def time_mix(i, li, x, carried, s0, B, T):
        vec = jnp.stack([rwkv_w0[li], rwkv_a0[li], rwkv_k_k[li], rwkv_k_a[li], rwkv_r_k[li].reshape(D)])
        outs = rwkv_prep(
            x, x if carried is None else carried, norm_mix[i], rwkv_mu[li], rwkv_w_rkv[li], rwkv_w1[li], rwkv_w2[li],
            rwkv_a1[li], rwkv_a2[li], rwkv_g1[li], rwkv_g2[li], vec, T, 0 if carried is None else T)
        if carried is None:
            v, gate, sc = outs[7:]
            y, st = wkv_scan_tokens(outs[:6], outs[6])
            y, st = y.reshape(B * T, D), _state_out(st, B)
        else:
            a, wr, w, b, k, v, gate, sc = outs
            y, st = _wkv_carried(a, wr, w, b, k, v, sc[:, :N_HEADS], sc[:, N_HEADS:2 * N_HEADS], s0, B, T)
        return rwkv_post(y, v, gate, sc, x, rwkv_w_o[li], rwkv_ln_w[li], rwkv_ln_b[li]), st

    depth = norm_mix.shape[0]
    for i in range(depth):
        li = i // 2
        if i % 2 == 0:
            p_shift.append(_rmsnorm(xp.reshape(Bp, Tp, D)[:, -1], norm_mix[i]))
            s_shift.append(_rmsnorm(xs.reshape(Bs, Ts, D)[:, -1], norm_mix[i]))
            xp, stp = time_mix(i, li, xp, None, None, Bp, Tp)
            xs, sts = time_mix(i, li, xs, jnp.repeat(state_rwkv_shift[li], Ts, axis=0), state_rwkv_wkv[li], Bs, Ts)
            p_wkv.append(stp)
            s_wkv.append(sts)
        else:
            hs = _rmsnorm(xs, norm_mix[i])
            assert len(caches_k[0].shape) == 5 and caches_k[0].shape[0] == 1 and li == 0
            hgs = norm_classes(xp, norm_mix[i], DILATIONS, Tp)
            outs, lses = [], []
            for g in range(N_GROUPS):
                dil = DILATIONS[g]
                L = Tp // dil
                keep = min(WINDOWS[g], Tp)
                assert Tp % (dil * BAND) == 0 and keep % dil == 0
                qkv_g = mm(hgs[g], attn_w_in[li][:, g * 3 * D:(g + 1) * 3 * D]).reshape(Bp * dil, L, 3 * D)
                pos = (jnp.arange(dil)[:, None] + dil * jnp.arange(L)[None, :]).reshape(-1)
                o, lse, kr = band_attn(qkv_g, *_rope_tables(pos), dil)
                outs.append(o)
                lses.append(lse)
                tail = lambda t: t.reshape(Bp, dil, L, D)[:, :, L - keep // dil:].transpose(0, 2, 1, 3).reshape(
                    Bp, keep, N_HEADS, HEAD_DIM)
                p_k[g].append(tail(kr))
                p_v[g].append(tail(qkv_g[..., 2 * D:]))
            qkv_s = mm(hs.astype(BF16), attn_w_in[li]).reshape(Bs, Ts, N_GROUPS, 3, N_HEADS, HEAD_DIM)
            pos_s = PAST_LEN + jnp.arange(Ts)
            q_s = _rope(qkv_s[:, :, :, 0], pos_s)
            k_s = _rope(qkv_s[:, :, :, 1], pos_s)
            v_s = qkv_s[:, :, :, 2]
            rows8 = lambda t: jnp.pad(t.transpose(0, 2, 3, 1, 4), ((0, 0),) * 3 + ((0, SUBLANES - Ts), (0, 0)))
            osmp = sample_attn(rows8(q_s), rows8(k_s), rows8(v_s), caches_k, caches_v, Ts)
            osmp = osmp[:, :, :Ts].transpose(0, 2, 1, 3).reshape(Ms, D)
            for g in range(N_GROUPS):
                s_k[g].append(k_s[:, :, g])
                s_v[g].append(v_s[:, :, g])
            xp = attn_merge(outs, lses, DILATIONS, xp, attn_w_o[li], Tp)
            xs = xs + mm(osmp, attn_w_o[li])

        buf = state_ffn_conv[i]
        first = (jnp.arange(Ms) % Ts == 0)[:, None]
        e1 = jnp.repeat(buf[:, 1], Ts, axis=0)
        e2 = jnp.where(first, jnp.repeat(buf[:, 0], Ts, axis=0), e1)
        tail = jnp.concatenate([xp.reshape(Bp, Tp, D)[:, Tp - 2:].reshape(Bp * 2, D),
                                xs.reshape(Bs, Ts, D)[:, Ts - 2:].reshape(Bs * 2, D)], axis=0)
        gate_tail = mm(_rmsnorm(tail, norm_ffn[i]), ffn_w_up[i][:, :D_FF])
        p_conv.append(gate_tail[:Bp * 2].reshape(Bp, 2, D_FF))
        s_conv.append(gate_tail[Bp * 2:].reshape(Bs, 2, D_FF))
        xp, xs = conv_glu_ffn(xp, xs, norm_ffn[i], ffn_w_up[i], ffn_conv_w[i], ffn_conv_b[i], ffn_w_down[i],
                              e1, e2, norm_final if i == depth - 1 else None, Tp, Ts)

    y_prompt = xp.reshape(Bp, Tp, D)
    y_sample = xs.reshape(Bs, Ts, D)
    st = jnp.stack
    return (y_prompt, y_sample,
            st(p_shift), st(p_wkv),
            st(p_k[0]), st(p_v[0]), st(p_k[1]), st(p_v[1]), st(p_k[2]), st(p_v[2]), st(p_conv),
            st(s_shift), st(s_wkv),
            st(s_k[0]), st(s_v[0]), st(s_k[1]), st(s_v[1]), st(s_k[2]), st(s_v[2]), st(s_conv))
```
